```python
import math
import jax
import jax.numpy as jnp
from jax import lax
import numpy as np

D_MODEL = 1024
BATCH = 8
SEQ = 4096
DEPTH = 1

CTX_LEN = 256
GRID_W = 64
EPS = 1e-6

DN_HEADS = 8
DN_DK = 128
DN_DV = 128
DN_QK = DN_HEADS * DN_DK
DN_V = DN_HEADS * DN_DV
DN_CONV = 3
DN_CHUNK = 64

ATTN_HEADS = 8
ATTN_KV_HEADS = 2
ATTN_GROUP = ATTN_HEADS // ATTN_KV_HEADS
ATTN_DH = 128
ATTN_Q = ATTN_HEADS * ATTN_DH
ATTN_KV = ATTN_KV_HEADS * ATTN_DH
Q_BLOCK = 128
ROPE_THETA = 10000.0
ROPE_FREQS = ATTN_DH // 4

FFN_HIDDEN = 2816
FFN_RES_WEIGHT = 0.5

N_MOD = 9

IN_SIZES = (2 * DN_QK + DN_V, DN_V, 4 * DN_HEADS, ATTN_Q, 2 * ATTN_KV, 2 * D_MODEL)
IN_W = 2 * DN_QK + DN_V + DN_V + 4 * DN_HEADS + ATTN_Q + 2 * ATTN_KV + 2 * D_MODEL

kernel_name = 'hybrid_deltanet_gqa_macaron_prefix_block'


def _rmsnorm(x, g):
    xf = x.astype(jnp.float32)
    y = xf * lax.rsqrt(jnp.mean(xf * xf, axis=-1, keepdims=True) + EPS)
    return (y * g.astype(jnp.float32)).astype(x.dtype)


def _l2norm(x):
    xf = x.astype(jnp.float32)
    return xf * lax.rsqrt(jnp.sum(xf * xf, axis=-1, keepdims=True) + EPS)


def _modulate(h, shift, scale):
    return h * (1.0 + scale) + shift


def _ffn_sublayer(x, shift, scale, gate, pre_g, post_g, w_gu, w_down):
    h = _modulate(_rmsnorm(x, pre_g), shift, scale)
    a, b = jnp.split(h @ w_gu, 2, axis=-1)
    y = (jax.nn.silu(a) * b) @ w_down
    return x + FFN_RES_WEIGHT * gate * _rmsnorm(y, post_g)


def _short_conv(u, w):
    l = u.shape[1]
    pad = DN_CONV // 2
    up = jnp.pad(u, ((0, 0), (pad, pad), (0, 0)))
    out = up[:, 0:l] * w[0]
    for j in range(1, DN_CONV):
        out = out + up[:, j:j + l] * w[j]
    return out


def _rope_tables(l, dtype):
    rows = l // GRID_W
    row = jnp.repeat(jnp.arange(rows, dtype=jnp.int32), GRID_W)
    col = jnp.tile(jnp.arange(GRID_W, dtype=jnp.int32), rows)
    pos = jnp.stack([row, col], axis=-1).astype(jnp.float32)
    inv_freq = ROPE_THETA ** (-jnp.arange(ROPE_FREQS, dtype=jnp.float32) / ROPE_FREQS)
    ang = pos[:, :, None] * inv_freq
    return jnp.cos(ang).astype(dtype), jnp.sin(ang).astype(dtype)


def _rope(x, cos, sin):
    xr = x.reshape(*x.shape[:-1], 2, 2, ROPE_FREQS)
    x1, x2 = xr[..., 0, :], xr[..., 1, :]
    c, s = cos[:, None], sin[:, None]
    out = jnp.stack([x1 * c - x2 * s, x2 * c + x1 * s], axis=-2)
    return out.reshape(x.shape)


def _delta_rule(q, k, v, beta, g, s0):
    b, h, l, _ = q.shape
    dv = v.shape[-1]
    n = l // DN_CHUNK

    def chunk(t):
        return jnp.moveaxis(t.reshape(b, h, n, DN_CHUNK, *t.shape[3:]), 2, 0)

    qc, kc, vc, bc, gc = chunk(q), chunk(k), chunk(v), chunk(beta), chunk(g)
    gcum = jnp.cumsum(gc, axis=-1)
    idx = jnp.arange(DN_CHUNK)
    lower_incl = idx[:, None] >= idx[None, :]
    strict = idx[:, None] > idx[None, :]
    decay = jnp.exp(jnp.where(lower_incl, gcum[..., :, None] - gcum[..., None, :], -jnp.inf))
    kb = kc * bc[..., None]
    l_mat = jnp.where(strict, jnp.einsum('nbhid,nbhjd->nbhij', kb, kc) * decay, 0.0)
    rhs = jnp.concatenate([vc * bc[..., None], kb * jnp.exp(gcum)[..., None]], axis=-1)
    sol = lax.linalg.triangular_solve(l_mat, rhs, left_side=True, lower=True, unit_diagonal=True)
    u, w = sol[..., :dv], sol[..., dv:]
    intra = jnp.where(lower_incl, jnp.einsum('nbhid,nbhjd->nbhij', qc, kc) * decay, 0.0)
    q_dec = qc * jnp.exp(gcum)[..., None]
    k_dec = kc * jnp.exp(gcum[..., -1:] - gcum)[..., None]
    chunk_decay = jnp.exp(gcum[..., -1])

    def step(s, xs):
        q_d, k_d, u_c, w_c, a_c, d_c = xs
        v_new = u_c - jnp.einsum('bhcd,bhde->bhce', w_c, s)
        o = jnp.einsum('bhcd,bhde->bhce', q_d, s) + jnp.einsum('bhij,bhje->bhie', a_c, v_new)
        s = s * d_c[..., None, None] + jnp.einsum('bhcd,bhce->bhde', k_d, v_new)
        return s, o

    s_final, o = lax.scan(step, s0, (q_dec, k_dec, u, w, intra, chunk_decay))
    return jnp.moveaxis(o, 0, 2).reshape(b, h, l, dv), s_final


def _flip(t):
    return jnp.flip(t, axis=2)


def _attend(q, k, v):
    b, lq = q.shape[:2]
    qg = q.reshape(b, lq, ATTN_KV_HEADS, ATTN_GROUP, ATTN_DH)
    s = jnp.einsum('bqkgd,bskd->bkgqs', qg, k).astype(jnp.float32) * (ATTN_DH ** -0.5)
    p = jax.nn.softmax(s, axis=-1).astype(v.dtype)
    o = jnp.einsum('bkgqs,bskd->bqkgd', p, v)
    return o.reshape(b, lq, ATTN_Q)


def _block_attention(q, k, v):
    b, l = q.shape[:2]
    nb = l // Q_BLOCK
    qb = jnp.moveaxis(q.reshape(b, nb, Q_BLOCK, ATTN_HEADS, ATTN_DH), 1, 0)
    o = lax.map(lambda blk: _attend(blk, k, v), qb)
    return jnp.moveaxis(o, 0, 1).reshape(b, l, ATTN_Q)


def _mixer_inputs(h, w_in, conv_w, a_log, dt_bias, q_norm, k_norm, rope):
    b, l, _ = h.shape
    split_at = np.cumsum(IN_SIZES)[:-1].tolist()
    qkv, z, ba, q_a, kv_a, gates = jnp.split(h @ w_in, split_at, axis=-1)
    qkv = jax.nn.silu(_short_conv(qkv, conv_w))
    q_d, k_d, v_d = jnp.split(qkv, [DN_QK, 2 * DN_QK], axis=-1)
    q_d = _l2norm(q_d.reshape(b, l, DN_HEADS, DN_DK)) * (DN_DK ** -0.5)
    k_d = _l2norm(k_d.reshape(b, l, DN_HEADS, DN_DK))
    v_d = v_d.reshape(b, l, DN_HEADS, DN_DV).astype(jnp.float32)
    ba = ba.astype(jnp.float32).reshape(b, l, 2, 2, DN_HEADS)
    beta = jax.nn.sigmoid(ba[:, :, 0])
    g = -jnp.exp(a_log.astype(jnp.float32)) * jax.nn.softplus(ba[:, :, 1] + dt_bias.astype(jnp.float32))
    dn = (jnp.swapaxes(q_d, 1, 2), jnp.swapaxes(k_d, 1, 2), jnp.swapaxes(v_d, 1, 2),
          jnp.moveaxis(beta, 1, 3), jnp.moveaxis(g, 1, 3))
    q_a = _rmsnorm(q_a.reshape(b, l, ATTN_HEADS, ATTN_DH), q_norm)
    k_a, v_a = jnp.split(kv_a, 2, axis=-1)
    k_a = _rmsnorm(k_a.reshape(b, l, ATTN_KV_HEADS, ATTN_DH), k_norm)
    v_a = v_a.reshape(b, l, ATTN_KV_HEADS, ATTN_DH)
    if rope is not None:
        q_a = _rope(q_a, *rope)
        k_a = _rope(k_a, *rope)
    return dn, (q_a, k_a, v_a), z, gates


def _merge(o_dn, z, o_at, gates, dn_norm, w_dn_out, w_attn_out, w_out):
    b, l, _ = z.shape
    o = _rmsnorm(jnp.swapaxes(o_dn, 1, 2), dn_norm).astype(z.dtype)
    o = o * jax.nn.silu(z.reshape(b, l, DN_HEADS, DN_DV))
    y_dn = o.reshape(b, l, DN_V) @ w_dn_out
    y_at = o_at @ w_attn_out
    g_dn, g_at = jnp.split(gates, 2, axis=-1)
    return (jax.nn.sigmoid(g_dn) * y_dn + jax.nn.sigmoid(g_at) * y_at) @ w_out


def _mixer(h, hc, w_in, conv_w, a_log, dt_bias, dn_norm, w_dn_out, q_norm, k_norm, w_attn_out, w_out, ctx_out):
    b = h.shape[0]
    rope = _rope_tables(h.shape[1], h.dtype)
    (q_d, k_d, v_d, beta, g), (q_a, k_a, v_a), z, gates = _mixer_inputs(
        h, w_in, conv_w, a_log, dt_bias, q_norm, k_norm, rope)
    (qc_d, kc_d, vc_d, betac, gc), (qc_a, kc_a, vc_a), zc, gatesc = _mixer_inputs(
        hc, w_in, conv_w, a_log, dt_bias, q_norm, k_norm, None)
    s0 = jnp.zeros((b, DN_HEADS, DN_DK, DN_DV), jnp.float32)
    oc_f, s_f = _delta_rule(qc_d, kc_d, vc_d, betac[:, 0], gc[:, 0], s0)
    oc_b, s_b = _delta_rule(_flip(qc_d), _flip(kc_d), _flip(vc_d), _flip(betac[:, 1]), _flip(gc[:, 1]), s0)
    o_f, _ = _delta_rule(q_d, k_d, v_d, beta[:, 0], g[:, 0], s_f)
    o_b, _ = _delta_rule(_flip(q_d), _flip(k_d), _flip(v_d), _flip(beta[:, 1]), _flip(g[:, 1]), s_b)
    o_dn = o_f + _flip(o_b)
    k_all = jnp.concatenate([kc_a, k_a], axis=1)
    v_all = jnp.concatenate([vc_a, v_a], axis=1)
    o_at = _block_attention(q_a, k_all, v_all)
    y = _merge(o_dn, z, o_at, gates, dn_norm, w_dn_out, w_attn_out, w_out)
    if not ctx_out:
        return y, None
    yc = _merge(oc_f + _flip(oc_b), zc, _attend(qc_a, kc_a, vc_a), gatesc, dn_norm, w_dn_out, w_attn_out, w_out)
    return y, yc


def setup_inputs(seed: int = 0) -> dict:
    key = jax.random.key(seed)
    ks = jax.random.split(key, 28)
    f32 = jnp.float32

    def nrm(i, shape, std):
        return std * jax.random.normal(ks[i], shape, f32)

    def gain(i, shape):
        return 1.0 + 0.1 * jax.random.normal(ks[i], shape, f32)

    dt = jnp.exp(jax.random.uniform(ks[15], (DEPTH, 2, DN_HEADS), f32, math.log(1e-3), math.log(1e-1)))
    return {
        'x': nrm(0, (BATCH, SEQ, D_MODEL), 1.0),
        'c': nrm(1, (BATCH, D_MODEL), 1.0),
        'ctx': nrm(2, (BATCH, CTX_LEN, D_MODEL), 1.0),
        'c_ctx': nrm(3, (D_MODEL,), 1.0),
        'w_mod': nrm(4, (DEPTH, D_MODEL, N_MOD * D_MODEL), 0.5 * D_MODEL ** -0.5),
        'b_mod': nrm(5, (DEPTH, N_MOD * D_MODEL), 0.01),
        'ffn1_pre': gain(6, (DEPTH, D_MODEL)),
        'ffn1_post': gain(7, (DEPTH, D_MODEL)),
        'ffn1_w_gu': nrm(8, (DEPTH, D_MODEL, 2 * FFN_HIDDEN), D_MODEL ** -0.5),
        'ffn1_w_down': nrm(9, (DEPTH, FFN_HIDDEN, D_MODEL), FFN_HIDDEN ** -0.5),
        'mix_pre': gain(10, (DEPTH, D_MODEL)),
        'mix_post': gain(11, (DEPTH, D_MODEL)),
        'w_in': nrm(12, (DEPTH, D_MODEL, IN_W), D_MODEL ** -0.5),
        'dn_conv': nrm(13, (DEPTH, DN_CONV, 2 * DN_QK + DN_V), DN_CONV ** -0.5),
        'dn_a_log': jnp.log(jax.random.uniform(ks[14], (DEPTH, 2, DN_HEADS), f32, 1.0, 16.0)),
        'dn_dt_bias': dt + jnp.log(-jnp.expm1(-dt)),
        'dn_out_norm': gain(16, (DEPTH, DN_DV)),
        'w_dn_out': nrm(17, (DEPTH, DN_V, D_MODEL), DN_V ** -0.5),
        'attn_q_norm': gain(18, (DEPTH, ATTN_DH)),
        'attn_k_norm': gain(19, (DEPTH, ATTN_DH)),
        'w_attn_out': nrm(20, (DEPTH, ATTN_Q, D_MODEL), ATTN_Q ** -0.5),
        'w_out': nrm(21, (DEPTH, D_MODEL, D_MODEL), D_MODEL ** -0.5),
        'ffn2_pre': gain(22, (DEPTH, D_MODEL)),
        'ffn2_post': gain(23, (DEPTH, D_MODEL)),
        'ffn2_w_gu': nrm(24, (DEPTH, D_MODEL, 2 * FFN_HIDDEN), D_MODEL ** -0.5),
        'ffn2_w_down': nrm(25, (DEPTH, FFN_HIDDEN, D_MODEL), FFN_HIDDEN ** -0.5),
    }


def reference(x, c, ctx, c_ctx, w_mod, b_mod, ffn1_pre, ffn1_post, ffn1_w_gu, ffn1_w_down,
              mix_pre, mix_post, w_in, dn_conv, dn_a_log, dn_dt_bias, dn_out_norm, w_dn_out,
              attn_q_norm, attn_k_norm, w_attn_out, w_out, ffn2_pre, ffn2_post, ffn2_w_gu, ffn2_w_down):
    cx = ctx
    for i in range(DEPTH):
        last = i == DEPTH - 1
        mod = jnp.split((jax.nn.silu(c) @ w_mod[i] + b_mod[i])[:, None, :], N_MOD, axis=-1)
        mod_c = jnp.split((jax.nn.silu(c_ctx) @ w_mod[i] + b_mod[i])[None, None, :], N_MOD, axis=-1)
        x = _ffn_sublayer(x, mod[0], mod[1], mod[2], ffn1_pre[i], ffn1_post[i], ffn1_w_gu[i], ffn1_w_down[i])
        cx = _ffn_sublayer(cx, mod_c[0], mod_c[1], mod_c[2], ffn1_pre[i], ffn1_post[i], ffn1_w_gu[i], ffn1_w_down[i])
        h = _modulate(_rmsnorm(x, mix_pre[i]), mod[3], mod[4])
        hc = _modulate(_rmsnorm(cx, mix_pre[i]), mod_c[3], mod_c[4])
        y, yc = _mixer(h, hc, w_in[i], dn_conv[i], dn_a_log[i], dn_dt_bias[i], dn_out_norm[i], w_dn_out[i],
                       attn_q_norm[i], attn_k_norm[i], w_attn_out[i], w_out[i], not last)
        x = x + mod[5] * _rmsnorm(y, mix_post[i])
        x = _ffn_sublayer(x, mod[6], mod[7], mod[8], ffn2_pre[i], ffn2_post[i], ffn2_w_gu[i], ffn2_w_down[i])
        if not last:
            cx = cx + mod_c[5] * _rmsnorm(yc, mix_post[i])
            cx = _ffn_sublayer(cx, mod_c[6], mod_c[7], mod_c[8], ffn2_pre[i], ffn2_post[i], ffn2_w_gu[i], ffn2_w_down[i])
    return x
```

```python
import functools

import jax
import jax.numpy as jnp
from jax import lax
from jax.experimental import pallas as pl
from jax.experimental.pallas import tpu as pltpu

F32 = jnp.float32
BF16 = jnp.bfloat16

EPS = 1e-6
GRID_W = 64
ROPE_THETA = 10000.0
N_MOD = 9
FFN_RES_WEIGHT = 0.5

HEAD_DIM = 128
DN_HEADS = 8
ATTN_HEADS = 8
ATTN_KV_HEADS = 2
ATTN_GROUP = ATTN_HEADS // ATTN_KV_HEADS
DN_CHUNK = 64
DN_WIN = 256
CHUNKS_PER_WIN = DN_WIN // DN_CHUNK
NEUMANN_STEPS = 5
FFN_HIDDEN_CHUNK = 256

VMEM_LIMIT_BYTES = 56 * 1024 * 1024


def _dot(a, b):
    return jnp.dot(a, b, preferred_element_type=F32)


def _dot_nt(a, b):
    return lax.dot_general(a, b, (((1,), (1,)), ((), ())), preferred_element_type=F32)


def _rms(x, g):
    return x * lax.rsqrt(jnp.mean(x * x, axis=-1, keepdims=True) + EPS) * g


def _params(*sem):
    return pltpu.CompilerParams(dimension_semantics=sem, vmem_limit_bytes=VMEM_LIMIT_BYTES)


def _mod_kernel(c_ref, w_ref, b_ref, o_ref):
    a = jax.nn.silu(c_ref[...])
    o_ref[...] = jnp.dot(a, w_ref[...], preferred_element_type=F32,
                         precision=lax.Precision.HIGHEST) + b_ref[...]


def _modulation(cc, w_mod, b_mod):
    rows, d = cc.shape
    n = w_mod.shape[1]
    tn = d
    return pl.pallas_call(
        _mod_kernel,
        grid=(n // tn,),
        in_specs=[pl.BlockSpec((rows, d), lambda j: (0, 0)),
                  pl.BlockSpec((d, tn), lambda j: (0, j)),
                  pl.BlockSpec((1, tn), lambda j: (0, j))],
        out_specs=pl.BlockSpec((rows, tn), lambda j: (0, j)),
        out_shape=jax.ShapeDtypeStruct((rows, n), F32),
        compiler_params=_params("arbitrary"),
        name="adaln_mod",
    )(cc, w_mod, b_mod.reshape(1, n))


def _ffn_kernel(x_ref, sh_ref, sc_ref, gt_ref, pre_ref, post_ref, wg_ref, wu_ref, wd_ref, o_ref, *, n_chunks):
    x = x_ref[0]
    h = _rms(x, pre_ref[...]) * (1.0 + sc_ref[0]) + sh_ref[0]
    hb = h.astype(BF16)
    acc = None
    for j in range(n_chunks):
        a = _dot(hb, wg_ref[j])
        b = _dot(hb, wu_ref[j])
        act = (jax.nn.silu(a) * b).astype(BF16)
        d = _dot(act, wd_ref[j])
        acc = d if acc is None else acc + d
    o_ref[0] = x + FFN_RES_WEIGHT * gt_ref[0] * _rms(acc, post_ref[...])


def _ffn(x, shift, scale, gate, pre_g, post_g, wg, wu, wd, tm):
    nb, l, d = x.shape
    n_chunks, _, hc = wg.shape
    row = pl.BlockSpec((1, tm, d), lambda b, i: (b, i, 0))
    per_b = pl.BlockSpec((1, 1, d), lambda b, i: (b, 0, 0))
    vec = pl.BlockSpec((1, d), lambda b, i: (0, 0))
    w_up = pl.BlockSpec((n_chunks, d, hc), lambda b, i: (0, 0, 0))
    w_dn = pl.BlockSpec((n_chunks, hc, d), lambda b, i: (0, 0, 0))
    return pl.pallas_call(
        functools.partial(_ffn_kernel, n_chunks=n_chunks),
        grid=(nb, l // tm),
        in_specs=[row, per_b, per_b, per_b, vec, vec, w_up, w_up, w_dn],
        out_specs=row,
        out_shape=jax.ShapeDtypeStruct((nb, l, d), F32),
        compiler_params=_params("parallel", "parallel"),
        name="ffn",
    )(x, shift, scale, gate, pre_g, post_g, wg, wu, wd)


def _ffn_weights(w_gu, w_down):
    d, two_h = w_gu.shape
    hid = two_h // 2
    n_chunks = hid // FFN_HIDDEN_CHUNK
    wgu = w_gu.astype(BF16)
    wg = wgu[:, :hid].reshape(d, n_chunks, FFN_HIDDEN_CHUNK).transpose(1, 0, 2)
    wu = wgu[:, hid:].reshape(d, n_chunks, FFN_HIDDEN_CHUNK).transpose(1, 0, 2)
    wd = w_down.astype(BF16).reshape(n_chunks, FFN_HIDDEN_CHUNK, d)
    return wg, wu, wd


def _head_norm_rope(r, gain, n_heads, scale, rope):
    out = []
    if rope is not None:
        cosf, sinf = rope
        lane = lax.broadcasted_iota(jnp.int32, (1, HEAD_DIM), 1)
        first_half = (lane % (HEAD_DIM // 2)) < (HEAD_DIM // 4)
    for hd in range(n_heads):
        xh = r[:, hd * HEAD_DIM:(hd + 1) * HEAD_DIM]
        xh = _rms(xh, gain)
        if rope is not None:
            up = pltpu.roll(xh, HEAD_DIM - HEAD_DIM // 4, axis=1)
            dn = pltpu.roll(xh, HEAD_DIM // 4, axis=1)
            xh = xh * cosf + jnp.where(first_half, up, dn) * sinf
        if scale != 1.0:
            xh = xh * scale
        out.append(xh)
    return out


def _inproj_kernel(*refs, use_rope):
    if use_rope:
        (x_ref, sh_ref, sc_ref, pre_ref, wqkv_ref, wz_ref, wq_ref, wk_ref, wvt_ref, wg_ref, wbat_ref,
         qn_ref, kn_ref, cos_ref, sin_ref, oqkv, oz, oq, ok, ovt, og, obat) = refs
        rope = (cos_ref[...], sin_ref[...])
    else:
        (x_ref, sh_ref, sc_ref, pre_ref, wqkv_ref, wz_ref, wq_ref, wk_ref, wvt_ref, wg_ref, wbat_ref,
         qn_ref, kn_ref, oqkv, oz, oq, ok, ovt, og, obat) = refs
        rope = None
    x = x_ref[0]
    d = x.shape[-1]
    hb = (_rms(x, pre_ref[...]) * (1.0 + sc_ref[0]) + sh_ref[0]).astype(BF16)
    for j in range(wqkv_ref.shape[1] // d):
        oqkv[0, :, j * d:(j + 1) * d] = _dot(hb, wqkv_ref[:, j * d:(j + 1) * d]).astype(BF16)
    oz[0] = _dot(hb, wz_ref[...]).astype(BF16)
    for j in range(wg_ref.shape[1] // d):
        og[0, :, j * d:(j + 1) * d] = _dot(hb, wg_ref[:, j * d:(j + 1) * d]).astype(BF16)
    q = _head_norm_rope(_dot(hb, wq_ref[...]), qn_ref[...], ATTN_HEADS, HEAD_DIM ** -0.5, rope)
    for hd in range(ATTN_HEADS):
        oq[0, :, hd * HEAD_DIM:(hd + 1) * HEAD_DIM] = q[hd].astype(BF16)
    k = _head_norm_rope(_dot(hb, wk_ref[...]), kn_ref[...], ATTN_KV_HEADS, 1.0, rope)
    for hd in range(ATTN_KV_HEADS):
        ok[0, :, hd * HEAD_DIM:(hd + 1) * HEAD_DIM] = k[hd].astype(BF16)
    ovt[0] = _dot_nt(wvt_ref[...], hb).astype(BF16)
    obat[0] = _dot_nt(wbat_ref[...], hb)


def _inproj(x, shift, scale, pre_g, w, q_norm, k_norm, rope, tm):
    nb, l, d = x.shape
    use_rope = rope is not None
    row = pl.BlockSpec((1, tm, d), lambda b, i: (b, i, 0))
    per_b = pl.BlockSpec((1, 1, d), lambda b, i: (b, 0, 0))

    def full(a):
        return pl.BlockSpec(a.shape, lambda b, i: (0,) * a.ndim)

    ins = [x, shift, scale, pre_g, w["qkv"], w["z"], w["q"], w["k"], w["vt"], w["g"], w["bat"], q_norm, k_norm]
    in_specs = [row, per_b, per_b] + [full(a) for a in ins[3:]]
    if use_rope:
        ins += list(rope)
        in_specs += [pl.BlockSpec((tm, HEAD_DIM), lambda b, i: (i, 0))] * 2
    n_qkv, n_z, n_q, n_k = w["qkv"].shape[1], w["z"].shape[1], w["q"].shape[1], w["k"].shape[1]
    n_vt, n_g, n_ba = w["vt"].shape[0], w["g"].shape[1], w["bat"].shape[0]

    def tok(n):
        return pl.BlockSpec((1, tm, n), lambda b, i: (b, i, 0))

    def chan(n):
        return pl.BlockSpec((1, n, tm), lambda b, i: (b, 0, i))

    out_shape = [jax.ShapeDtypeStruct((nb, l, n_qkv), BF16), jax.ShapeDtypeStruct((nb, l, n_z), BF16),
                 jax.ShapeDtypeStruct((nb, l, n_q), BF16), jax.ShapeDtypeStruct((nb, l, n_k), BF16),
                 jax.ShapeDtypeStruct((nb, n_vt, l), BF16), jax.ShapeDtypeStruct((nb, l, n_g), BF16),
                 jax.ShapeDtypeStruct((nb, n_ba, l), F32)]
    out_specs = [tok(n_qkv), tok(n_z), tok(n_q), tok(n_k), chan(n_vt), tok(n_g), chan(n_ba)]
    return pl.pallas_call(
        functools.partial(_inproj_kernel, use_rope=use_rope),
        grid=(nb, l // tm),
        in_specs=in_specs,
        out_specs=out_specs,
        out_shape=out_shape,
        compiler_params=_params("parallel", "parallel"),
        name="mixer_inproj",
    )(*ins)


def _softplus(x):
    return jnp.maximum(x, 0.0) + jnp.log1p(jnp.exp(-jnp.abs(x)))


def _split3(x):
    x1 = x.astype(BF16)
    r1 = x - x1.astype(F32)
    x2 = r1.astype(BF16)
    x3 = (r1 - x2.astype(F32)).astype(BF16)
    return x1, x2, x3


def _lane_rep(row):
    return jnp.transpose(jnp.broadcast_to(row, (HEAD_DIM, DN_WIN)))


def _dn_kernel(alog_ref, dtb_ref, qx_ref, kx_ref, vx_ref, qc_ref, kc_ref, vc_ref,
               cwq_ref, cwk_ref, cwv_ref, rows_ref, o_ref,
               stg, qs, ks, vs, oacc, rq, masks, *, lc, l):
    z = lc + l
    n_win = z // DN_WIN
    n_cwin = lc // DN_WIN
    hd = pl.program_id(1)

    ri = lax.broadcasted_iota(jnp.int32, (DN_WIN, DN_WIN), 0)
    ci = lax.broadcasted_iota(jnp.int32, (DN_WIN, DN_WIN), 1)
    same = (ri // DN_CHUNK) == (ci // DN_CHUNK)
    masks[0] = (same & (ri >= ci)).astype(F32)
    masks[1] = (same & (ri > ci)).astype(F32)
    masks[2] = (same & (ri <= ci)).astype(F32)
    masks[3] = (same & (ri < ci)).astype(F32)
    masks[4] = (ri == ci).astype(F32)

    neg_a = [-jnp.exp(jnp.full((1, DN_WIN), alog_ref[dr, hd], F32)) for dr in range(2)]
    dtb = [jnp.full((1, DN_WIN), dtb_ref[dr, hd], F32) for dr in range(2)]
    for w in range(n_win):
        r = rows_ref[0, 0, :, w * DN_WIN:(w + 1) * DN_WIN]
        beta = [jax.nn.sigmoid(r[dr:dr + 1]) for dr in range(2)]
        g = [neg_a[dr] * _softplus(r[2 + dr:3 + dr] + dtb[dr]) for dr in range(2)]
        parts = _split3(g[0]) + _split3(g[1])
        lhs = jnp.concatenate(list(parts) + [jnp.zeros((10, DN_WIN), BF16)], axis=0)
        pre = _dot(lhs, masks[2].astype(BF16))
        suf = _dot(lhs, masks[0].astype(BF16))
        pre = [pre[0:1] + pre[1:2] + pre[2:3], pre[3:4] + pre[4:5] + pre[5:6]]
        suf = [suf[0:1] + suf[1:2] + suf[2:3], suf[3:4] + suf[4:5] + suf[5:6]]
        gcum = [pre[0], suf[1]]
        rest = [suf[0] - g[0], pre[1] - g[1]]
        rows = []
        for dr in range(2):
            gam = jnp.exp(gcum[dr])
            rows += [beta[dr], gcum[dr], gam, beta[dr] * gam, jnp.exp(rest[dr]),
                     jnp.exp(gcum[dr] + rest[dr])]
        rows.append(jnp.zeros((4, DN_WIN), F32))
        rq[w] = jnp.concatenate(rows, axis=0)

    zeros8 = jnp.zeros((8, HEAD_DIM), F32)
    tpos = lax.broadcasted_iota(jnp.int32, (DN_WIN, 1), 0)

    def conv_into(dst, c_ref, x_ref, cw_ref, norm_scale):
        stg[0:8, :] = zeros8
        stg[8:8 + lc, :] = c_ref[0].astype(F32)
        stg[8 + lc:8 + z, :] = x_ref[0].astype(F32)
        stg[8 + z:16 + z, :] = zeros8
        cw = cw_ref[...]

        def blk(i, carry):
            r0 = pl.multiple_of(i * DN_WIN, DN_WIN)
            win = stg[pl.ds(r0, DN_WIN + 16), :]
            t = tpos + r0
            prev = jnp.where((t == 0) | (t == lc), 0.0, win[7:7 + DN_WIN])
            nxt = jnp.where((t == lc - 1) | (t == z - 1), 0.0, win[9:9 + DN_WIN])
            y = prev * cw[0:1] + win[8:8 + DN_WIN] * cw[1:2] + nxt * cw[2:3]
            y = jax.nn.silu(y)
            if norm_scale is not None:
                y = y * lax.rsqrt(jnp.sum(y * y, axis=-1, keepdims=True) + EPS)
                if norm_scale != 1.0:
                    y = y * norm_scale
            dst[pl.ds(r0, DN_WIN), :] = y
            return carry

        lax.fori_loop(0, n_win, blk, 0)

    conv_into(qs, qc_ref, qx_ref, cwq_ref, HEAD_DIM ** -0.5)
    conv_into(ks, kc_ref, kx_ref, cwk_ref, 1.0)
    conv_into(vs, vc_ref, vx_ref, cwv_ref, None)

    oacc[...] = jnp.zeros((z, HEAD_DIM), F32)

    lane_chunk = lax.broadcasted_iota(jnp.int32, (1, DN_WIN), 1) // DN_CHUNK

    def run_window(w, dr, s):
        r0 = pl.multiple_of(w * DN_WIN, DN_WIN)
        kw = ks[pl.ds(r0, DN_WIN), :]
        qw = qs[pl.ds(r0, DN_WIN), :]
        vw = vs[pl.ds(r0, DN_WIN), :]
        rqw = rq[w]
        b0 = 6 * dr
        beta = _lane_rep(rqw[b0:b0 + 1])
        grow = rqw[b0 + 1:b0 + 2]
        gcol = _lane_rep(grow)
        gam = _lane_rep(rqw[b0 + 2:b0 + 3])
        bgam = _lane_rep(rqw[b0 + 3:b0 + 4])
        kdm = _lane_rep(rqw[b0 + 4:b0 + 5])
        dch = _lane_rep(rqw[b0 + 5:b0 + 6])
        incl = masks[0] if dr == 0 else masks[2]
        strict = masks[1] if dr == 0 else masks[3]

        kb = kw.astype(BF16)
        kk = _dot_nt(kb, kb)
        qk = _dot_nt(qw.astype(BF16), kb)
        gdiff = jnp.concatenate([gcol, gcol], axis=1) - grow
        e = jnp.exp(jnp.minimum(gdiff, 0.0))
        beta2 = jnp.concatenate([beta, beta], axis=1)
        m = -(kk * e * beta2 * strict)
        am = qk * e * incl

        t = masks[4] + m
        mp = m
        for _ in range(NEUMANN_STEPS):
            mpb = mp.astype(BF16)
            mp = _dot(mpb, mpb)
            t = t + _dot(t.astype(BF16), mp.astype(BF16))

        rhs = jnp.concatenate([vw * beta, kw * bgam], axis=1).astype(BF16)
        uwb = _dot(t.astype(BF16), rhs).astype(BF16)
        auw = _dot(am.astype(BF16), uwb)
        o0 = auw[:, :HEAD_DIM]
        qe = (qw * gam - auw[:, HEAD_DIM:]).astype(BF16)
        kdt = jnp.transpose(kw * kdm)

        order = range(CHUNKS_PER_WIN) if dr == 0 else range(CHUNKS_PER_WIN - 1, -1, -1)
        for c in order:
            lo = c * DN_CHUNK
            kdt_c = jnp.where(lane_chunk == c, kdt, 0.0).astype(BF16)
            pr = _dot(kdt_c, uwb)
            lhs = jnp.concatenate([qe[lo:lo + DN_CHUNK], (-pr[:, HEAD_DIM:]).astype(BF16)], axis=0)
            res = _dot(lhs, s.astype(BF16))
            rows = pl.ds(r0 + lo, DN_CHUNK)
            oacc[rows, :] = oacc[rows, :] + o0[lo:lo + DN_CHUNK] + res[:DN_CHUNK]
            dvec = jnp.concatenate([dch[lo:lo + DN_CHUNK]] * 2, axis=0)
            s = dvec * s + res[DN_CHUNK:] + pr[:, :HEAD_DIM]
        return s

    def step(i, carry):
        s_f, s_b = carry
        w_b = jnp.where(i < n_cwin, n_cwin - 1 - i, n_win - 1 - (i - n_cwin))
        s_f = run_window(i, 0, s_f)
        s_b = run_window(w_b, 1, s_b)
        return s_f, s_b

    s0 = jnp.zeros((HEAD_DIM, HEAD_DIM), F32)
    lax.fori_loop(0, n_win, step, (s0, s0))

    o_ref[0] = oacc[lc:z, :].astype(BF16)


def _deltanet(qkv_x, qkv_c, conv_w, rows, a_log, dt_bias):
    nb, l, _ = qkv_x.shape
    lc = qkv_c.shape[1]
    z = lc + l
    n_win = z // DN_WIN
    nh = DN_HEADS

    def col(seq, off):
        return pl.BlockSpec((1, seq, HEAD_DIM), lambda b, h: (b, 0, off + h))

    def cw(off):
        return pl.BlockSpec((conv_w.shape[0], HEAD_DIM), lambda b, h: (0, off + h))

    smem = pl.BlockSpec(memory_space=pltpu.SMEM)
    return pl.pallas_call(
        functools.partial(_dn_kernel, lc=lc, l=l),
        grid=(nb, nh),
        in_specs=[smem, smem,
                  col(l, 0), col(l, nh), col(l, 2 * nh), col(lc, 0), col(lc, nh), col(lc, 2 * nh),
                  cw(0), cw(nh), cw(2 * nh),
                  pl.BlockSpec((1, 1, 8, z), lambda b, h: (b, h, 0, 0))],
        out_specs=pl.BlockSpec((1, l, HEAD_DIM), lambda b, h: (b, 0, h)),
        out_shape=jax.ShapeDtypeStruct((nb, l, nh * HEAD_DIM), BF16),
        scratch_shapes=[pltpu.VMEM((z + 16, HEAD_DIM), F32),
                        pltpu.VMEM((z, HEAD_DIM), F32),
                        pltpu.VMEM((z, HEAD_DIM), F32),
                        pltpu.VMEM((z, HEAD_DIM), F32),
                        pltpu.VMEM((z, HEAD_DIM), F32),
                        pltpu.VMEM((n_win, 16, DN_WIN), F32),
                        pltpu.VMEM((5, DN_WIN, DN_WIN), F32)],
        compiler_params=_params("parallel", "parallel"),
        name="deltanet",
    )(a_log, dt_bias, qkv_x, qkv_x, qkv_x, qkv_c, qkv_c, qkv_c, conv_w, conv_w, conv_w, rows)


def _attn_kernel(q_ref, k_ref, vt_ref, o_ref, *, tq, tk, n_kblk):
    q = q_ref[0]
    qm = jnp.concatenate([q[:, g * HEAD_DIM:(g + 1) * HEAD_DIM] for g in range(ATTN_GROUP)], axis=0)
    ncol = ATTN_GROUP * tq

    def body(j, carry):
        m, lsum, acc = carry
        k0 = pl.multiple_of(j * tk, tk)
        s = _dot_nt(k_ref[0, pl.ds(k0, tk), :], qm)
        m_new = jnp.maximum(m, jnp.max(s, axis=0, keepdims=True))
        p = jnp.exp(s - m_new)
        alpha = jnp.exp(m - m_new)
        lsum = alpha * lsum + jnp.sum(p, axis=0, keepdims=True)
        acc = alpha * acc + _dot(vt_ref[0, :, pl.ds(k0, tk)], p.astype(BF16))
        return m_new, lsum, acc

    init = (jnp.full((1, ncol), -jnp.inf, F32), jnp.zeros((1, ncol), F32), jnp.zeros((HEAD_DIM, ncol), F32))
    _, lsum, acc = lax.fori_loop(0, n_kblk, body, init)
    ot = acc / lsum
    for g in range(ATTN_GROUP):
        o_ref[0, :, g * HEAD_DIM:(g + 1) * HEAD_DIM] = jnp.transpose(ot[:, g * tq:(g + 1) * tq]).astype(BF16)


def _attention(q, k_all, vt_all, tq, tk):
    nb, l, _ = q.shape
    z = k_all.shape[1]
    gw = ATTN_GROUP * HEAD_DIM
    return pl.pallas_call(
        functools.partial(_attn_kernel, tq=tq, tk=tk, n_kblk=z // tk),
        grid=(nb, ATTN_KV_HEADS, l // tq),
        in_specs=[pl.BlockSpec((1, tq, gw), lambda b, kv, i: (b, i, kv)),
                  pl.BlockSpec((1, z, HEAD_DIM), lambda b, kv, i: (b, 0, kv)),
                  pl.BlockSpec((1, HEAD_DIM, z), lambda b, kv, i: (b, kv, 0))],
        out_specs=pl.BlockSpec((1, tq, gw), lambda b, kv, i: (b, i, kv)),
        out_shape=jax.ShapeDtypeStruct((nb, l, ATTN_HEADS * HEAD_DIM), BF16),
        compiler_params=_params("parallel", "parallel", "parallel"),
        name="gqa_attn",
    )(q, k_all, vt_all)


def _merge_kernel(x_ref, gate_ref, odn_ref, z_ref, oat_ref, g_ref, dnn_ref, post_ref,
                  wdn_ref, wat_ref, wo_ref, o_ref):
    x = x_ref[0]
    d = x.shape[-1]
    odn = odn_ref[0].astype(F32)
    zz = z_ref[0].astype(F32)
    parts = []
    for hd in range(DN_HEADS):
        sl = slice(hd * HEAD_DIM, (hd + 1) * HEAD_DIM)
        parts.append((_rms(odn[:, sl], dnn_ref[...]) * jax.nn.silu(zz[:, sl])).astype(BF16))
    y_dn = _dot(jnp.concatenate(parts, axis=1), wdn_ref[...])
    y_at = _dot(oat_ref[0], wat_ref[...])
    g = g_ref[0].astype(F32)
    y = jax.nn.sigmoid(g[:, :d]) * y_dn + jax.nn.sigmoid(g[:, d:]) * y_at
    yy = _dot(y.astype(BF16), wo_ref[...])
    o_ref[0] = x + gate_ref[0] * _rms(yy, post_ref[...])


def _merge(x, gate, o_dn, z, o_at, gates, dn_norm, post_g, w_dn, w_at, w_o, tm):
    nb, l, d = x.shape

    def tok(n):
        return pl.BlockSpec((1, tm, n), lambda b, i: (b, i, 0))

    def full(a):
        return pl.BlockSpec(a.shape, lambda b, i: (0,) * a.ndim)

    return pl.pallas_call(
        _merge_kernel,
        grid=(nb, l // tm),
        in_specs=[tok(d), pl.BlockSpec((1, 1, d), lambda b, i: (b, 0, 0)), tok(o_dn.shape[-1]), tok(z.shape[-1]),
                  tok(o_at.shape[-1]), tok(gates.shape[-1]), full(dn_norm), full(post_g),
                  full(w_dn), full(w_at), full(w_o)],
        out_specs=tok(d),
        out_shape=jax.ShapeDtypeStruct((nb, l, d), F32),
        compiler_params=_params("parallel", "parallel"),
        name="mixer_merge",
    )(x, gate, o_dn, z, o_at, gates, dn_norm, post_g, w_dn, w_at, w_o)


def _rope_tables(l):
    rows = l // GRID_W
    row = jnp.repeat(jnp.arange(rows, dtype=jnp.int32), GRID_W)
    colp = jnp.tile(jnp.arange(GRID_W, dtype=jnp.int32), rows)
    pos = jnp.stack([row, colp], axis=-1).astype(F32)
    n_freq = HEAD_DIM // 4
    inv_freq = ROPE_THETA ** (-jnp.arange(n_freq, dtype=F32) / n_freq)
    ang = pos[:, :, None] * inv_freq
    cos, sin = jnp.cos(ang), jnp.sin(ang)
    cosf = jnp.concatenate([cos[:, 0], cos[:, 0], cos[:, 1], cos[:, 1]], axis=-1)
    sinf = jnp.concatenate([-sin[:, 0], sin[:, 0], -sin[:, 1], sin[:, 1]], axis=-1)
    return cosf, sinf


def _split_w_in(w_in):
    d = w_in.shape[0]
    dn_qkv = 3 * DN_HEADS * HEAD_DIM
    dn_v = DN_HEADS * HEAD_DIM
    n_ba = 4 * DN_HEADS
    a_q = ATTN_HEADS * HEAD_DIM
    a_kv = ATTN_KV_HEADS * HEAD_DIM
    sizes = (dn_qkv, dn_v, n_ba, a_q, a_kv, a_kv, 2 * d)
    offs = [0]
    for sz in sizes:
        offs.append(offs[-1] + sz)
    assert offs[-1] == w_in.shape[1]
    wb = w_in.astype(BF16)
    part = [wb[:, offs[i]:offs[i + 1]] for i in range(len(sizes))]
    return {"qkv": part[0], "z": part[1], "bat": part[2].T, "q": part[3], "k": part[4],
            "vt": part[5].T, "g": part[6]}


def kernel(x, c, ctx, c_ctx, w_mod, b_mod, ffn1_pre, ffn1_post, ffn1_w_gu, ffn1_w_down, mix_pre, mix_post, w_in, dn_conv, dn_a_log, dn_dt_bias, dn_out_norm, w_dn_out, attn_q_norm, attn_k_norm, w_attn_out, w_out, ffn2_pre, ffn2_post, ffn2_w_gu, ffn2_w_down):
    nb, l, d = x.shape
    lc = ctx.shape[1]
    assert w_mod.shape[0] == 1, "single-layer block: the context stream is never updated"
    assert l % DN_WIN == 0 and lc % DN_WIN == 0 and l % GRID_W == 0
    tm_x = min(512, l)
    tm_c = min(256, lc)

    pad = (-(nb + 1)) % 8
    cc = jnp.concatenate([c, c_ctx[None, :], jnp.zeros((pad, d), F32)], axis=0)
    mod_all = _modulation(cc, w_mod[0], b_mod[0])
    mod = [mod_all[:nb, j * d:(j + 1) * d][:, None, :] for j in range(N_MOD)]
    mod_c = [jnp.broadcast_to(mod_all[nb:nb + 1, j * d:(j + 1) * d][:, None, :], (nb, 1, d)) for j in range(N_MOD)]

    f1 = _ffn_weights(ffn1_w_gu[0], ffn1_w_down[0])
    x1 = _ffn(x, mod[0], mod[1], mod[2], ffn1_pre, ffn1_post, *f1, tm=tm_x)
    c1 = _ffn(ctx, mod_c[0], mod_c[1], mod_c[2], ffn1_pre, ffn1_post, *f1, tm=tm_c)

    w = _split_w_in(w_in[0])
    rope = _rope_tables(l)
    qkv_x, z_x, qa_x, ka_x, vt_x, g_x, bat_x = _inproj(
        x1, mod[3], mod[4], mix_pre, w, attn_q_norm, attn_k_norm, rope, tm=min(256, l))
    qkv_c, _, _, ka_c, vt_c, _, bat_c = _inproj(
        c1, mod_c[3], mod_c[4], mix_pre, w, attn_q_norm, attn_k_norm, None, tm=tm_c)

    z = lc + l
    bat = jnp.concatenate([bat_c, bat_x], axis=-1)
    rows = bat.reshape(nb, 2, 2, DN_HEADS, z).transpose(0, 3, 1, 2, 4).reshape(nb, DN_HEADS, 4, z)
    rows = jnp.concatenate([rows, jnp.zeros_like(rows)], axis=2)
    o_dn = _deltanet(qkv_x, qkv_c, dn_conv[0], rows, dn_a_log[0], dn_dt_bias[0])

    k_all = jnp.concatenate([ka_c, ka_x], axis=1)
    vt_all = jnp.concatenate([vt_c, vt_x], axis=2)
    o_at = _attention(qa_x, k_all, vt_all, tq=min(256, l), tk=DN_WIN)

    x2 = _merge(x1, mod[5], o_dn, z_x, o_at, g_x, dn_out_norm, mix_post,
                w_dn_out[0].astype(BF16), w_attn_out[0].astype(BF16), w_out[0].astype(BF16), tm=tm_x)

    f2 = _ffn_weights(ffn2_w_gu[0], ffn2_w_down[0])
    return _ffn(x2, mod[6], mod[7], mod[8], ffn2_pre, ffn2_post, *f2, tm=tm_x)
```

```python
import functools

import jax
import jax.numpy as jnp
from jax import lax
from jax.experimental import pallas as pl
from jax.experimental.pallas import tpu as pltpu

F32 = jnp.float32
BF16 = jnp.bfloat16

EPS = 1e-6
GRID_W = 64
ROPE_THETA = 10000.0
N_MOD = 9
FFN_RES_WEIGHT = 0.5

HEAD_DIM = 128
DN_HEADS = 8
ATTN_HEADS = 8
ATTN_KV_HEADS = 2
ATTN_GROUP = ATTN_HEADS // ATTN_KV_HEADS
DN_CHUNK = 64
DN_WIN = 256
CHUNKS_PER_WIN = DN_WIN // DN_CHUNK
DN_HEADS_PER_STEP = 2
NEUMANN_STEPS = 5
FFN_HIDDEN_CHUNK = 256
ATTN_Q_SCALE = HEAD_DIM ** -0.5 * 1.4426950408889634

VMEM_LIMIT_BYTES = 56 * 1024 * 1024


def _dot(a, b):
    return jnp.dot(a, b, preferred_element_type=F32)


def _dot_nt(a, b):
    return lax.dot_general(a, b, (((1,), (1,)), ((), ())), preferred_element_type=F32)


def _rms(x, g):
    return x * lax.rsqrt(jnp.mean(x * x, axis=-1, keepdims=True) + EPS) * g


def _params(*sem):
    return pltpu.CompilerParams(dimension_semantics=sem, vmem_limit_bytes=VMEM_LIMIT_BYTES)


def _mod_kernel(c_ref, w_ref, b_ref, o_ref):
    a = jax.nn.silu(c_ref[...])
    o_ref[...] = jnp.dot(a, w_ref[...], preferred_element_type=F32,
                         precision=lax.Precision.HIGHEST) + b_ref[...]


def _modulation(cc, w_mod, b_mod):
    rows, d = cc.shape
    n = w_mod.shape[1]
    tn = d
    return pl.pallas_call(
        _mod_kernel,
        grid=(n // tn,),
        in_specs=[pl.BlockSpec((rows, d), lambda j: (0, 0)),
                  pl.BlockSpec((d, tn), lambda j: (0, j)),
                  pl.BlockSpec((1, tn), lambda j: (0, j))],
        out_specs=pl.BlockSpec((rows, tn), lambda j: (0, j)),
        out_shape=jax.ShapeDtypeStruct((rows, n), F32),
        compiler_params=_params("arbitrary"),
        name="adaln_mod",
    )(cc, w_mod, b_mod.reshape(1, n))


def _ffn_kernel(x_ref, sh_ref, sc_ref, gt_ref, pre_ref, post_ref, wg_ref, wu_ref, wd_ref, o_ref, *, n_chunks):
    x = x_ref[0]
    h = _rms(x, pre_ref[...]) * (1.0 + sc_ref[0]) + sh_ref[0]
    hb = h.astype(BF16)
    acc = None
    for j in range(n_chunks):
        a = _dot(hb, wg_ref[j])
        b = _dot(hb, wu_ref[j])
        act = (jax.nn.silu(a) * b).astype(BF16)
        d = _dot(act, wd_ref[j])
        acc = d if acc is None else acc + d
    o_ref[0] = x + FFN_RES_WEIGHT * gt_ref[0] * _rms(acc, post_ref[...])


def _ffn(x, shift, scale, gate, pre_g, post_g, wg, wu, wd, tm):
    nb, l, d = x.shape
    n_chunks, _, hc = wg.shape
    row = pl.BlockSpec((1, tm, d), lambda b, i: (b, i, 0))
    per_b = pl.BlockSpec((1, 1, d), lambda b, i: (b, 0, 0))
    vec = pl.BlockSpec((1, d), lambda b, i: (0, 0))
    w_up = pl.BlockSpec((n_chunks, d, hc), lambda b, i: (0, 0, 0))
    w_dn = pl.BlockSpec((n_chunks, hc, d), lambda b, i: (0, 0, 0))
    return pl.pallas_call(
        functools.partial(_ffn_kernel, n_chunks=n_chunks),
        grid=(nb, l // tm),
        in_specs=[row, per_b, per_b, per_b, vec, vec, w_up, w_up, w_dn],
        out_specs=row,
        out_shape=jax.ShapeDtypeStruct((nb, l, d), F32),
        compiler_params=_params("parallel", "parallel"),
        name="ffn",
    )(x, shift, scale, gate, pre_g, post_g, wg, wu, wd)


def _ffn_weights(w_gu, w_down):
    d, two_h = w_gu.shape
    hid = two_h // 2
    n_chunks = hid // FFN_HIDDEN_CHUNK
    wgu = w_gu.astype(BF16)
    wg = wgu[:, :hid].reshape(d, n_chunks, FFN_HIDDEN_CHUNK).transpose(1, 0, 2)
    wu = wgu[:, hid:].reshape(d, n_chunks, FFN_HIDDEN_CHUNK).transpose(1, 0, 2)
    wd = w_down.astype(BF16).reshape(n_chunks, FFN_HIDDEN_CHUNK, d)
    return wg, wu, wd


def _head_norm_rope(r, gain, n_heads, scale, rope):
    out = []
    if rope is not None:
        cosf, sinf = rope
        lane = lax.broadcasted_iota(jnp.int32, (1, HEAD_DIM), 1)
        first_half = (lane % (HEAD_DIM // 2)) < (HEAD_DIM // 4)
    for hd in range(n_heads):
        xh = r[:, hd * HEAD_DIM:(hd + 1) * HEAD_DIM]
        xh = _rms(xh, gain)
        if rope is not None:
            up = pltpu.roll(xh, HEAD_DIM - HEAD_DIM // 4, axis=1)
            dn = pltpu.roll(xh, HEAD_DIM // 4, axis=1)
            xh = xh * cosf + jnp.where(first_half, up, dn) * sinf
        if scale != 1.0:
            xh = xh * scale
        out.append(xh)
    return out


def _inproj_kernel(*refs, use_rope):
    if use_rope:
        (x_ref, sh_ref, sc_ref, pre_ref, wqkv_ref, wz_ref, wq_ref, wk_ref, wvt_ref, wg_ref, wbat_ref,
         qn_ref, kn_ref, cos_ref, sin_ref, oqkv, oz, oq, ok, ovt, og, obat) = refs
        rope = (cos_ref[...], sin_ref[...])
    else:
        (x_ref, sh_ref, sc_ref, pre_ref, wqkv_ref, wz_ref, wq_ref, wk_ref, wvt_ref, wg_ref, wbat_ref,
         qn_ref, kn_ref, oqkv, oz, oq, ok, ovt, og, obat) = refs
        rope = None
    x = x_ref[0]
    d = x.shape[-1]
    hb = (_rms(x, pre_ref[...]) * (1.0 + sc_ref[0]) + sh_ref[0]).astype(BF16)
    for j in range(wqkv_ref.shape[1] // d):
        oqkv[0, :, j * d:(j + 1) * d] = _dot(hb, wqkv_ref[:, j * d:(j + 1) * d]).astype(BF16)
    oz[0] = _dot(hb, wz_ref[...]).astype(BF16)
    for j in range(wg_ref.shape[1] // d):
        og[0, :, j * d:(j + 1) * d] = _dot(hb, wg_ref[:, j * d:(j + 1) * d]).astype(BF16)
    q = _head_norm_rope(_dot(hb, wq_ref[...]), qn_ref[...], ATTN_HEADS, ATTN_Q_SCALE, rope)
    for hd in range(ATTN_HEADS):
        oq[0, :, hd * HEAD_DIM:(hd + 1) * HEAD_DIM] = q[hd].astype(BF16)
    k = _head_norm_rope(_dot(hb, wk_ref[...]), kn_ref[...], ATTN_KV_HEADS, 1.0, rope)
    for hd in range(ATTN_KV_HEADS):
        ok[0, :, hd * HEAD_DIM:(hd + 1) * HEAD_DIM] = k[hd].astype(BF16)
    ovt[0] = _dot_nt(wvt_ref[...], hb).astype(BF16)
    obat[0] = _dot_nt(wbat_ref[...], hb)


def _inproj(x, shift, scale, pre_g, w, q_norm, k_norm, rope, tm):
    nb, l, d = x.shape
    use_rope = rope is not None
    row = pl.BlockSpec((1, tm, d), lambda b, i: (b, i, 0))
    per_b = pl.BlockSpec((1, 1, d), lambda b, i: (b, 0, 0))

    def full(a):
        return pl.BlockSpec(a.shape, lambda b, i: (0,) * a.ndim)

    ins = [x, shift, scale, pre_g, w["qkv"], w["z"], w["q"], w["k"], w["vt"], w["g"], w["bat"], q_norm, k_norm]
    in_specs = [row, per_b, per_b] + [full(a) for a in ins[3:]]
    if use_rope:
        ins += list(rope)
        in_specs += [pl.BlockSpec((tm, HEAD_DIM), lambda b, i: (i, 0))] * 2
    n_qkv, n_z, n_q, n_k = w["qkv"].shape[1], w["z"].shape[1], w["q"].shape[1], w["k"].shape[1]
    n_vt, n_g, n_ba = w["vt"].shape[0], w["g"].shape[1], w["bat"].shape[0]

    def tok(n):
        return pl.BlockSpec((1, tm, n), lambda b, i: (b, i, 0))

    def chan(n):
        return pl.BlockSpec((1, n, tm), lambda b, i: (b, 0, i))

    out_shape = [jax.ShapeDtypeStruct((nb, l, n_qkv), BF16), jax.ShapeDtypeStruct((nb, l, n_z), BF16),
                 jax.ShapeDtypeStruct((nb, l, n_q), BF16), jax.ShapeDtypeStruct((nb, l, n_k), BF16),
                 jax.ShapeDtypeStruct((nb, n_vt, l), BF16), jax.ShapeDtypeStruct((nb, l, n_g), BF16),
                 jax.ShapeDtypeStruct((nb, n_ba, l), F32)]
    out_specs = [tok(n_qkv), tok(n_z), tok(n_q), tok(n_k), chan(n_vt), tok(n_g), chan(n_ba)]
    return pl.pallas_call(
        functools.partial(_inproj_kernel, use_rope=use_rope),
        grid=(nb, l // tm),
        in_specs=in_specs,
        out_specs=out_specs,
        out_shape=out_shape,
        compiler_params=_params("parallel", "parallel"),
        name="mixer_inproj",
    )(*ins)


def _softplus(x):
    return jnp.maximum(x, 0.0) + jnp.log1p(jnp.exp(-jnp.abs(x)))


def _split3(x):
    x1 = x.astype(BF16)
    r1 = x - x1.astype(F32)
    x2 = r1.astype(BF16)
    x3 = (r1 - x2.astype(F32)).astype(BF16)
    return x1, x2, x3


def _lane_rep(row):
    return jnp.transpose(jnp.broadcast_to(row, (HEAD_DIM, DN_WIN)))


def _dn_kernel(alog_ref, dtb_ref, qx_ref, kx_ref, vx_ref, qc_ref, kc_ref, vc_ref,
               cwq_ref, cwk_ref, cwv_ref, rows_ref, o_ref,
               stg, qs, ks, vs, oacc, rq, masks, qe_s, o0_s, dd_s, pp_s, rr_s, *, lc, l, hp):
    z = lc + l
    n_win = z // DN_WIN
    n_cwin = lc // DN_WIN
    hd0 = pl.program_id(1) * hp

    ri = lax.broadcasted_iota(jnp.int32, (DN_WIN, DN_WIN), 0)
    ci = lax.broadcasted_iota(jnp.int32, (DN_WIN, DN_WIN), 1)
    same = (ri // DN_CHUNK) == (ci // DN_CHUNK)
    masks[0] = (same & (ri >= ci)).astype(F32)
    masks[1] = (same & (ri > ci)).astype(F32)
    masks[2] = (same & (ri <= ci)).astype(F32)
    masks[3] = (same & (ri < ci)).astype(F32)
    masks[4] = (ri == ci).astype(F32)

    for hh in range(hp):
        neg_a = [-jnp.exp(jnp.full((1, DN_WIN), alog_ref[dr, hd0 + hh], F32)) for dr in range(2)]
        dtb = [jnp.full((1, DN_WIN), dtb_ref[dr, hd0 + hh], F32) for dr in range(2)]
        for w in range(n_win):
            r = rows_ref[0, hh, :, w * DN_WIN:(w + 1) * DN_WIN]
            beta = [jax.nn.sigmoid(r[dr:dr + 1]) for dr in range(2)]
            g = [neg_a[dr] * _softplus(r[2 + dr:3 + dr] + dtb[dr]) for dr in range(2)]
            parts = _split3(g[0]) + _split3(g[1])
            lhs = jnp.concatenate(list(parts) + [jnp.zeros((10, DN_WIN), BF16)], axis=0)
            pre = _dot(lhs, masks[2].astype(BF16))
            suf = _dot(lhs, masks[0].astype(BF16))
            pre = [pre[0:1] + pre[1:2] + pre[2:3], pre[3:4] + pre[4:5] + pre[5:6]]
            suf = [suf[0:1] + suf[1:2] + suf[2:3], suf[3:4] + suf[4:5] + suf[5:6]]
            gcum = [pre[0], suf[1]]
            rest = [suf[0] - g[0], pre[1] - g[1]]
            rows = []
            for dr in range(2):
                gam = jnp.exp(gcum[dr])
                rows += [beta[dr], gcum[dr], gam, beta[dr] * gam, jnp.exp(rest[dr]),
                         jnp.exp(gcum[dr] + rest[dr])]
            rows.append(jnp.zeros((4, DN_WIN), F32))
            rq[hh, w] = jnp.concatenate(rows, axis=0)

    zeros8 = jnp.zeros((8, HEAD_DIM), F32)
    tpos = lax.broadcasted_iota(jnp.int32, (DN_WIN, 1), 0)

    def conv_into(dst, hh, c_ref, x_ref, cw_ref, norm_scale):
        cols = slice(hh * HEAD_DIM, (hh + 1) * HEAD_DIM)
        stg[0:8, :] = zeros8
        stg[8:8 + lc, :] = c_ref[0, :, cols].astype(F32)
        stg[8 + lc:8 + z, :] = x_ref[0, :, cols].astype(F32)
        stg[8 + z:16 + z, :] = zeros8
        cw = cw_ref[:, cols]

        def blk(i, carry):
            r0 = pl.multiple_of(i * DN_WIN, DN_WIN)
            win = stg[pl.ds(r0, DN_WIN + 16), :]
            t = tpos + r0
            prev = jnp.where((t == 0) | (t == lc), 0.0, win[7:7 + DN_WIN])
            nxt = jnp.where((t == lc - 1) | (t == z - 1), 0.0, win[9:9 + DN_WIN])
            y = prev * cw[0:1] + win[8:8 + DN_WIN] * cw[1:2] + nxt * cw[2:3]
            y = jax.nn.silu(y)
            if norm_scale is not None:
                y = y * lax.rsqrt(jnp.sum(y * y, axis=-1, keepdims=True) + EPS)
                if norm_scale != 1.0:
                    y = y * norm_scale
            dst[hh, pl.ds(r0, DN_WIN), :] = y
            return carry

        lax.fori_loop(0, n_win, blk, 0)

    for hh in range(hp):
        conv_into(qs, hh, qc_ref, qx_ref, cwq_ref, HEAD_DIM ** -0.5)
        conv_into(ks, hh, kc_ref, kx_ref, cwk_ref, 1.0)
        conv_into(vs, hh, vc_ref, vx_ref, cwv_ref, None)

    oacc[...] = jnp.zeros(oacc.shape, F32)

    lane_chunk = lax.broadcasted_iota(jnp.int32, (1, DN_WIN), 1) // DN_CHUNK
    streams = [(hh, dr) for hh in range(hp) for dr in range(2)]

    def bwd_window(i):
        return jnp.where(i < n_cwin, n_cwin - 1 - i, n_win - 1 - (i - n_cwin))

    zero_slot = 1
    qe_s[zero_slot] = jnp.zeros(qe_s.shape[1:], BF16)
    o0_s[zero_slot] = jnp.zeros(o0_s.shape[1:], F32)
    dd_s[zero_slot] = jnp.zeros(dd_s.shape[1:], F32)
    pp_s[zero_slot] = jnp.zeros(pp_s.shape[1:], BF16)
    rr_s[zero_slot] = jnp.zeros(rr_s.shape[1:], F32)

    def step(i, carry):
        state = list(carry)
        cur = jnp.minimum(i, n_win - 1)
        prev = jnp.maximum(i - 1, 0)
        slot, pslot = i % 2, (i + 1) % 2
        wins = (cur, bwd_window(cur))
        r0 = [pl.multiple_of(w * DN_WIN, DN_WIN) for w in wins]
        pr0 = [pl.multiple_of(w * DN_WIN, DN_WIN) for w in (prev, bwd_window(prev))]

        def recur_step(k):
            cs = (k, CHUNKS_PER_WIN - 1 - k)
            blk = DN_CHUNK + HEAD_DIM
            for hh in range(hp):
                lhs = []
                for dr in range(2):
                    lo = cs[dr] * DN_CHUNK
                    lhs += [qe_s[pslot, hh, dr, lo:lo + DN_CHUNK, :], pp_s[pslot, hh, dr, cs[dr]]]
                s_pair = state[2 * hh:2 * hh + 2]
                res = _dot(jnp.concatenate(lhs, axis=0), jnp.concatenate(s_pair, axis=1).astype(BF16))
                for dr in range(2):
                    lo = cs[dr] * DN_CHUNK
                    sub = res[dr * blk:(dr + 1) * blk, dr * HEAD_DIM:(dr + 1) * HEAD_DIM]
                    rows = pl.ds(pr0[dr] + lo, DN_CHUNK)
                    oacc[hh, rows, :] = (oacc[hh, rows, :] + o0_s[pslot, hh, dr, lo:lo + DN_CHUNK, :]
                                         + sub[:DN_CHUNK])
                    dv = dd_s[pslot, hh, dr, lo:lo + DN_CHUNK, :]
                    state[2 * hh + dr] = (jnp.concatenate([dv, dv], axis=0) * s_pair[dr] + sub[DN_CHUNK:]
                                          + rr_s[pslot, hh, dr, cs[dr]])

        st = []
        for hh, dr in streams:
            rqw = rq[hh, wins[dr]]
            b0 = 6 * dr
            st.append(dict(
                hh=hh, dr=dr, rqw=rqw, b0=b0,
                kw=ks[hh, pl.ds(r0[dr], DN_WIN), :],
                qw=qs[hh, pl.ds(r0[dr], DN_WIN), :],
                vw=vs[hh, pl.ds(r0[dr], DN_WIN), :],
                beta=_lane_rep(rqw[b0:b0 + 1]),
                gcol=_lane_rep(rqw[b0 + 1:b0 + 2])))
        for x in st:
            kb = x["kw"].astype(BF16)
            x["kk"] = _dot_nt(kb, kb)
            x["qk"] = _dot_nt(x["qw"].astype(BF16), kb)
        recur_step(0)
        for x in st:
            b0, rqw = x["b0"], x["rqw"]
            incl = masks[0] if x["dr"] == 0 else masks[2]
            strict = masks[1] if x["dr"] == 0 else masks[3]
            gdiff = jnp.concatenate([x["gcol"]] * 2, axis=1) - rqw[b0 + 1:b0 + 2]
            e = jnp.exp(jnp.minimum(gdiff, 0.0))
            x["am"] = (x["qk"] * e * incl).astype(BF16)
            m = -(x["kk"] * e * jnp.concatenate([x["beta"]] * 2, axis=1) * strict)
            x["t"] = masks[4] + m
            x["mp"] = m
        for n in range(NEUMANN_STEPS):
            for x in st:
                mpb = x["mp"].astype(BF16)
                x["mp"] = _dot(mpb, mpb)
            for x in st:
                x["t"] = x["t"] + _dot(x["t"].astype(BF16), x["mp"].astype(BF16))
            if n % 2 == 0 and 1 + n // 2 < CHUNKS_PER_WIN:
                recur_step(1 + n // 2)
        for x in st:
            b0, rqw = x["b0"], x["rqw"]
            bgam = _lane_rep(rqw[b0 + 3:b0 + 4])
            rhs = jnp.concatenate([x["vw"] * x["beta"], x["kw"] * bgam], axis=1).astype(BF16)
            x["uwb"] = _dot(x["t"].astype(BF16), rhs).astype(BF16)
        for x in st:
            hh, dr, b0, rqw = x["hh"], x["dr"], x["b0"], x["rqw"]
            auw = _dot(x["am"], x["uwb"])
            o0_s[slot, hh, dr] = auw[:, :HEAD_DIM]
            qe_s[slot, hh, dr] = (x["qw"] * _lane_rep(rqw[b0 + 2:b0 + 3]) - auw[:, HEAD_DIM:]).astype(BF16)
            dd_s[slot, hh, dr] = _lane_rep(rqw[b0 + 5:b0 + 6])
            x["kdt"] = jnp.transpose(x["kw"] * _lane_rep(rqw[b0 + 4:b0 + 5]))
        for c in range(CHUNKS_PER_WIN):
            for x in st:
                kdt_c = jnp.where(lane_chunk == c, x["kdt"], 0.0).astype(BF16)
                pr = _dot(kdt_c, x["uwb"])
                rr_s[slot, x["hh"], x["dr"], c] = pr[:, :HEAD_DIM]
                pp_s[slot, x["hh"], x["dr"], c] = (-pr[:, HEAD_DIM:]).astype(BF16)
        return tuple(state)

    s0 = jnp.zeros((HEAD_DIM, HEAD_DIM), F32)
    lax.fori_loop(0, n_win + 1, step, (s0,) * len(streams))

    for hh in range(hp):
        o_ref[0, :, hh * HEAD_DIM:(hh + 1) * HEAD_DIM] = oacc[hh, lc:z, :].astype(BF16)


def _deltanet(qkv_x, qkv_c, conv_w, rows, a_log, dt_bias):
    nb, l, _ = qkv_x.shape
    lc = qkv_c.shape[1]
    z = lc + l
    n_win = z // DN_WIN
    hp = DN_HEADS_PER_STEP
    ng = DN_HEADS // hp
    gw = hp * HEAD_DIM

    def col(seq, off):
        return pl.BlockSpec((1, seq, gw), lambda b, h: (b, 0, off + h))

    def cw(off):
        return pl.BlockSpec((conv_w.shape[0], gw), lambda b, h: (0, off + h))

    smem = pl.BlockSpec(memory_space=pltpu.SMEM)
    slot_shape = (2, hp, 2)
    return pl.pallas_call(
        functools.partial(_dn_kernel, lc=lc, l=l, hp=hp),
        grid=(nb, ng),
        in_specs=[smem, smem,
                  col(l, 0), col(l, ng), col(l, 2 * ng), col(lc, 0), col(lc, ng), col(lc, 2 * ng),
                  cw(0), cw(ng), cw(2 * ng),
                  pl.BlockSpec((1, hp, 8, z), lambda b, h: (b, h, 0, 0))],
        out_specs=pl.BlockSpec((1, l, gw), lambda b, h: (b, 0, h)),
        out_shape=jax.ShapeDtypeStruct((nb, l, DN_HEADS * HEAD_DIM), BF16),
        scratch_shapes=[pltpu.VMEM((z + 16, HEAD_DIM), F32),
                        pltpu.VMEM((hp, z, HEAD_DIM), F32),
                        pltpu.VMEM((hp, z, HEAD_DIM), F32),
                        pltpu.VMEM((hp, z, HEAD_DIM), F32),
                        pltpu.VMEM((hp, z, HEAD_DIM), F32),
                        pltpu.VMEM((hp, n_win, 16, DN_WIN), F32),
                        pltpu.VMEM((5, DN_WIN, DN_WIN), F32),
                        pltpu.VMEM(slot_shape + (DN_WIN, HEAD_DIM), BF16),
                        pltpu.VMEM(slot_shape + (DN_WIN, HEAD_DIM), F32),
                        pltpu.VMEM(slot_shape + (DN_WIN, HEAD_DIM), F32),
                        pltpu.VMEM(slot_shape + (CHUNKS_PER_WIN, HEAD_DIM, HEAD_DIM), BF16),
                        pltpu.VMEM(slot_shape + (CHUNKS_PER_WIN, HEAD_DIM, HEAD_DIM), F32)],
        compiler_params=_params("parallel", "parallel"),
        name="deltanet",
    )(a_log, dt_bias, qkv_x, qkv_x, qkv_x, qkv_c, qkv_c, qkv_c, conv_w, conv_w, conv_w, rows)


def _attn_kernel(q_ref, k_ref, vt_ref, o_ref, *, tq, tk, n_kblk):
    q = q_ref[0]
    qm = jnp.concatenate([q[:, g * HEAD_DIM:(g + 1) * HEAD_DIM] for g in range(ATTN_GROUP)], axis=0)
    ncol = ATTN_GROUP * tq

    m = jnp.full((1, ncol), -jnp.inf, F32)
    lsum = jnp.zeros((1, ncol), F32)
    acc = jnp.zeros((HEAD_DIM, ncol), F32)
    for j in range(n_kblk):
        s = _dot_nt(k_ref[0, j * tk:(j + 1) * tk, :], qm)
        m_new = jnp.maximum(m, jnp.max(s, axis=0, keepdims=True))
        p = jnp.exp2(s - m_new)
        alpha = jnp.exp2(m - m_new)
        lsum = alpha * lsum + jnp.sum(p, axis=0, keepdims=True)
        acc = alpha * acc + _dot(vt_ref[0, :, j * tk:(j + 1) * tk], p.astype(BF16))
        m = m_new
    ot = acc / lsum
    for g in range(ATTN_GROUP):
        o_ref[0, :, g * HEAD_DIM:(g + 1) * HEAD_DIM] = jnp.transpose(ot[:, g * tq:(g + 1) * tq]).astype(BF16)


def _attention(q, k_all, vt_all, tq, tk):
    nb, l, _ = q.shape
    z = k_all.shape[1]
    gw = ATTN_GROUP * HEAD_DIM
    return pl.pallas_call(
        functools.partial(_attn_kernel, tq=tq, tk=tk, n_kblk=z // tk),
        grid=(nb, ATTN_KV_HEADS, l // tq),
        in_specs=[pl.BlockSpec((1, tq, gw), lambda b, kv, i: (b, i, kv)),
                  pl.BlockSpec((1, z, HEAD_DIM), lambda b, kv, i: (b, 0, kv)),
                  pl.BlockSpec((1, HEAD_DIM, z), lambda b, kv, i: (b, kv, 0))],
        out_specs=pl.BlockSpec((1, tq, gw), lambda b, kv, i: (b, i, kv)),
        out_shape=jax.ShapeDtypeStruct((nb, l, ATTN_HEADS * HEAD_DIM), BF16),
        compiler_params=_params("parallel", "parallel", "parallel"),
        name="gqa_attn",
    )(q, k_all, vt_all)


def _merge_kernel(x_ref, gate_ref, odn_ref, z_ref, oat_ref, g_ref, dnn_ref, post_ref,
                  wdn_ref, wat_ref, wo_ref, o_ref):
    x = x_ref[0]
    d = x.shape[-1]
    odn = odn_ref[0].astype(F32)
    zz = z_ref[0].astype(F32)
    parts = []
    for hd in range(DN_HEADS):
        sl = slice(hd * HEAD_DIM, (hd + 1) * HEAD_DIM)
        parts.append((_rms(odn[:, sl], dnn_ref[...]) * jax.nn.silu(zz[:, sl])).astype(BF16))
    y_dn = _dot(jnp.concatenate(parts, axis=1), wdn_ref[...])
    y_at = _dot(oat_ref[0], wat_ref[...])
    g = g_ref[0].astype(F32)
    y = jax.nn.sigmoid(g[:, :d]) * y_dn + jax.nn.sigmoid(g[:, d:]) * y_at
    yy = _dot(y.astype(BF16), wo_ref[...])
    o_ref[0] = x + gate_ref[0] * _rms(yy, post_ref[...])


def _merge(x, gate, o_dn, z, o_at, gates, dn_norm, post_g, w_dn, w_at, w_o, tm):
    nb, l, d = x.shape

    def tok(n):
        return pl.BlockSpec((1, tm, n), lambda b, i: (b, i, 0))

    def full(a):
        return pl.BlockSpec(a.shape, lambda b, i: (0,) * a.ndim)

    return pl.pallas_call(
        _merge_kernel,
        grid=(nb, l // tm),
        in_specs=[tok(d), pl.BlockSpec((1, 1, d), lambda b, i: (b, 0, 0)), tok(o_dn.shape[-1]), tok(z.shape[-1]),
                  tok(o_at.shape[-1]), tok(gates.shape[-1]), full(dn_norm), full(post_g),
                  full(w_dn), full(w_at), full(w_o)],
        out_specs=tok(d),
        out_shape=jax.ShapeDtypeStruct((nb, l, d), F32),
        compiler_params=_params("parallel", "parallel"),
        name="mixer_merge",
    )(x, gate, o_dn, z, o_at, gates, dn_norm, post_g, w_dn, w_at, w_o)


def _rope_tables(l):
    rows = l // GRID_W
    row = jnp.repeat(jnp.arange(rows, dtype=jnp.int32), GRID_W)
    colp = jnp.tile(jnp.arange(GRID_W, dtype=jnp.int32), rows)
    pos = jnp.stack([row, colp], axis=-1).astype(F32)
    n_freq = HEAD_DIM // 4
    inv_freq = ROPE_THETA ** (-jnp.arange(n_freq, dtype=F32) / n_freq)
    ang = pos[:, :, None] * inv_freq
    cos, sin = jnp.cos(ang), jnp.sin(ang)
    cosf = jnp.concatenate([cos[:, 0], cos[:, 0], cos[:, 1], cos[:, 1]], axis=-1)
    sinf = jnp.concatenate([-sin[:, 0], sin[:, 0], -sin[:, 1], sin[:, 1]], axis=-1)
    return cosf, sinf


def _split_w_in(w_in):
    d = w_in.shape[0]
    dn_qkv = 3 * DN_HEADS * HEAD_DIM
    dn_v = DN_HEADS * HEAD_DIM
    n_ba = 4 * DN_HEADS
    a_q = ATTN_HEADS * HEAD_DIM
    a_kv = ATTN_KV_HEADS * HEAD_DIM
    sizes = (dn_qkv, dn_v, n_ba, a_q, a_kv, a_kv, 2 * d)
    offs = [0]
    for sz in sizes:
        offs.append(offs[-1] + sz)
    assert offs[-1] == w_in.shape[1]
    wb = w_in.astype(BF16)
    part = [wb[:, offs[i]:offs[i + 1]] for i in range(len(sizes))]
    return {"qkv": part[0], "z": part[1], "bat": part[2].T, "q": part[3], "k": part[4],
            "vt": part[5].T, "g": part[6]}


def kernel(x, c, ctx, c_ctx, w_mod, b_mod, ffn1_pre, ffn1_post, ffn1_w_gu, ffn1_w_down, mix_pre, mix_post, w_in, dn_conv, dn_a_log, dn_dt_bias, dn_out_norm, w_dn_out, attn_q_norm, attn_k_norm, w_attn_out, w_out, ffn2_pre, ffn2_post, ffn2_w_gu, ffn2_w_down):
    nb, l, d = x.shape
    lc = ctx.shape[1]
    assert w_mod.shape[0] == 1, "single-layer block: the context stream is never updated"
    assert l % DN_WIN == 0 and lc % DN_WIN == 0 and l % GRID_W == 0
    tm_x = min(512, l)
    tm_c = min(256, lc)

    pad = (-(nb + 1)) % 8
    cc = jnp.concatenate([c, c_ctx[None, :], jnp.zeros((pad, d), F32)], axis=0)
    mod_all = _modulation(cc, w_mod[0], b_mod[0])
    mod = [mod_all[:nb, j * d:(j + 1) * d][:, None, :] for j in range(N_MOD)]
    mod_c = [jnp.broadcast_to(mod_all[nb:nb + 1, j * d:(j + 1) * d][:, None, :], (nb, 1, d)) for j in range(N_MOD)]

    f1 = _ffn_weights(ffn1_w_gu[0], ffn1_w_down[0])
    x1 = _ffn(x, mod[0], mod[1], mod[2], ffn1_pre, ffn1_post, *f1, tm=tm_x)
    c1 = _ffn(ctx, mod_c[0], mod_c[1], mod_c[2], ffn1_pre, ffn1_post, *f1, tm=tm_c)

    w = _split_w_in(w_in[0])
    rope = _rope_tables(l)
    qkv_x, z_x, qa_x, ka_x, vt_x, g_x, bat_x = _inproj(
        x1, mod[3], mod[4], mix_pre, w, attn_q_norm, attn_k_norm, rope, tm=min(256, l))
    qkv_c, _, _, ka_c, vt_c, _, bat_c = _inproj(
        c1, mod_c[3], mod_c[4], mix_pre, w, attn_q_norm, attn_k_norm, None, tm=tm_c)

    z = lc + l
    bat = jnp.concatenate([bat_c, bat_x], axis=-1)
    rows = bat.reshape(nb, 2, 2, DN_HEADS, z).transpose(0, 3, 1, 2, 4).reshape(nb, DN_HEADS, 4, z)
    rows = jnp.concatenate([rows, jnp.zeros_like(rows)], axis=2)
    o_dn = _deltanet(qkv_x, qkv_c, dn_conv[0], rows, dn_a_log[0], dn_dt_bias[0])

    k_all = jnp.concatenate([ka_c, ka_x], axis=1)
    vt_all = jnp.concatenate([vt_c, vt_x], axis=2)
    o_at = _attention(qa_x, k_all, vt_all, tq=min(256, l), tk=DN_WIN)

    x2 = _merge(x1, mod[5], o_dn, z_x, o_at, g_x, dn_out_norm, mix_post,
                w_dn_out[0].astype(BF16), w_attn_out[0].astype(BF16), w_out[0].astype(BF16), tm=tm_x)

    f2 = _ffn_weights(ffn2_w_gu[0], ffn2_w_down[0])
    return _ffn(x2, mod[6], mod[7], mod[8], ffn2_pre, ffn2_post, *f2, tm=tm_x)
```

```python
import functools

import jax
import jax.numpy as jnp
from jax import lax
from jax.experimental import pallas as pl
from jax.experimental.pallas import tpu as pltpu

F32 = jnp.float32
BF16 = jnp.bfloat16

EPS = 1e-6
GRID_W = 64
ROPE_THETA = 10000.0
N_MOD = 9
FFN_RES_WEIGHT = 0.5

HEAD_DIM = 128
DN_HEADS = 8
ATTN_HEADS = 8
ATTN_KV_HEADS = 2
ATTN_GROUP = ATTN_HEADS // ATTN_KV_HEADS
DN_CHUNK = 64
DN_WIN = 256
CHUNKS_PER_WIN = DN_WIN // DN_CHUNK
DN_HEADS_PER_STEP = 2
INVERSE_LEVELS = 6
FFN_HIDDEN_CHUNK = 256
ATTN_SCORE_LOOKAHEAD = 2
ATTN_ONES_ROWS = 16
ATTN_Q_SCALE = HEAD_DIM ** -0.5 * 1.4426950408889634

VMEM_LIMIT_BYTES = 56 * 1024 * 1024


def _dot(a, b):
    return jnp.dot(a, b, preferred_element_type=F32)


def _dot_nt(a, b):
    return lax.dot_general(a, b, (((1,), (1,)), ((), ())), preferred_element_type=F32)


def _rms(x, g):
    return x * lax.rsqrt(jnp.mean(x * x, axis=-1, keepdims=True) + EPS) * g


def _params(*sem):
    return pltpu.CompilerParams(dimension_semantics=sem, vmem_limit_bytes=VMEM_LIMIT_BYTES)


def _mod_kernel(c_ref, w_ref, b_ref, o_ref):
    a = jax.nn.silu(c_ref[...])
    o_ref[...] = jnp.dot(a, w_ref[...], preferred_element_type=F32,
                         precision=lax.Precision.HIGHEST) + b_ref[...]


def _modulation(cc, w_mod, b_mod):
    rows, d = cc.shape
    n = w_mod.shape[1]
    tn = d
    return pl.pallas_call(
        _mod_kernel,
        grid=(n // tn,),
        in_specs=[pl.BlockSpec((rows, d), lambda j: (0, 0)),
                  pl.BlockSpec((d, tn), lambda j: (0, j)),
                  pl.BlockSpec((1, tn), lambda j: (0, j))],
        out_specs=pl.BlockSpec((rows, tn), lambda j: (0, j)),
        out_shape=jax.ShapeDtypeStruct((rows, n), F32),
        compiler_params=_params("arbitrary"),
        name="adaln_mod",
    )(cc, w_mod, b_mod.reshape(1, n))


def _ffn_kernel(x_ref, sh_ref, sc_ref, gt_ref, pre_ref, post_ref, wg_ref, wu_ref, wd_ref, o_ref, *, n_chunks):
    x = x_ref[0]
    h = _rms(x, pre_ref[...]) * (1.0 + sc_ref[0]) + sh_ref[0]
    hb = h.astype(BF16)
    acc = None
    for j in range(n_chunks):
        a = _dot(hb, wg_ref[j])
        b = _dot(hb, wu_ref[j])
        act = (jax.nn.silu(a) * b).astype(BF16)
        d = _dot(act, wd_ref[j])
        acc = d if acc is None else acc + d
    o_ref[0] = x + FFN_RES_WEIGHT * gt_ref[0] * _rms(acc, post_ref[...])


def _ffn(x, shift, scale, gate, pre_g, post_g, wg, wu, wd, tm):
    nb, l, d = x.shape
    n_chunks, _, hc = wg.shape
    row = pl.BlockSpec((1, tm, d), lambda b, i: (b, i, 0))
    per_b = pl.BlockSpec((1, 1, d), lambda b, i: (b, 0, 0))
    vec = pl.BlockSpec((1, d), lambda b, i: (0, 0))
    w_up = pl.BlockSpec((n_chunks, d, hc), lambda b, i: (0, 0, 0))
    w_dn = pl.BlockSpec((n_chunks, hc, d), lambda b, i: (0, 0, 0))
    return pl.pallas_call(
        functools.partial(_ffn_kernel, n_chunks=n_chunks),
        grid=(nb, l // tm),
        in_specs=[row, per_b, per_b, per_b, vec, vec, w_up, w_up, w_dn],
        out_specs=row,
        out_shape=jax.ShapeDtypeStruct((nb, l, d), F32),
        compiler_params=_params("parallel", "parallel"),
        name="ffn",
    )(x, shift, scale, gate, pre_g, post_g, wg, wu, wd)


def _ffn_weights(w_gu, w_down):
    d, two_h = w_gu.shape
    hid = two_h // 2
    n_chunks = hid // FFN_HIDDEN_CHUNK
    wgu = w_gu.astype(BF16)
    wg = wgu[:, :hid].reshape(d, n_chunks, FFN_HIDDEN_CHUNK).transpose(1, 0, 2)
    wu = wgu[:, hid:].reshape(d, n_chunks, FFN_HIDDEN_CHUNK).transpose(1, 0, 2)
    wd = w_down.astype(BF16).reshape(n_chunks, FFN_HIDDEN_CHUNK, d)
    return wg, wu, wd


def _head_norm_rope(r, gain, n_heads, scale, rope):
    out = []
    if rope is not None:
        cosf, sinf = rope
        lane = lax.broadcasted_iota(jnp.int32, (1, HEAD_DIM), 1)
        first_half = (lane % (HEAD_DIM // 2)) < (HEAD_DIM // 4)
    for hd in range(n_heads):
        xh = r[:, hd * HEAD_DIM:(hd + 1) * HEAD_DIM]
        xh = _rms(xh, gain)
        if rope is not None:
            up = pltpu.roll(xh, HEAD_DIM - HEAD_DIM // 4, axis=1)
            dn = pltpu.roll(xh, HEAD_DIM // 4, axis=1)
            xh = xh * cosf + jnp.where(first_half, up, dn) * sinf
        if scale != 1.0:
            xh = xh * scale
        out.append(xh)
    return out


def _inproj_kernel(*refs, use_rope):
    if use_rope:
        (x_ref, sh_ref, sc_ref, pre_ref, wqkv_ref, wz_ref, wq_ref, wk_ref, wvt_ref, wg_ref, wbat_ref,
         qn_ref, kn_ref, cos_ref, sin_ref, oqkv, oz, oq, ok, ovt, og, obat) = refs
        rope = (cos_ref[...], sin_ref[...])
    else:
        (x_ref, sh_ref, sc_ref, pre_ref, wqkv_ref, wz_ref, wq_ref, wk_ref, wvt_ref, wg_ref, wbat_ref,
         qn_ref, kn_ref, oqkv, oz, oq, ok, ovt, og, obat) = refs
        rope = None
    x = x_ref[0]
    d = x.shape[-1]
    hb = (_rms(x, pre_ref[...]) * (1.0 + sc_ref[0]) + sh_ref[0]).astype(BF16)
    for j in range(wqkv_ref.shape[1] // d):
        oqkv[0, :, j * d:(j + 1) * d] = _dot(hb, wqkv_ref[:, j * d:(j + 1) * d]).astype(BF16)
    oz[0] = _dot(hb, wz_ref[...]).astype(BF16)
    for j in range(wg_ref.shape[1] // d):
        og[0, :, j * d:(j + 1) * d] = _dot(hb, wg_ref[:, j * d:(j + 1) * d]).astype(BF16)
    q = _head_norm_rope(_dot(hb, wq_ref[...]), qn_ref[...], ATTN_HEADS, ATTN_Q_SCALE, rope)
    for hd in range(ATTN_HEADS):
        oq[0, :, hd * HEAD_DIM:(hd + 1) * HEAD_DIM] = q[hd].astype(BF16)
    k = _head_norm_rope(_dot(hb, wk_ref[...]), kn_ref[...], ATTN_KV_HEADS, 1.0, rope)
    for hd in range(ATTN_KV_HEADS):
        ok[0, :, hd * HEAD_DIM:(hd + 1) * HEAD_DIM] = k[hd].astype(BF16)
    ovt[0] = _dot_nt(wvt_ref[...], hb).astype(BF16)
    obat[0] = _dot_nt(wbat_ref[...], hb)


def _inproj(x, shift, scale, pre_g, w, q_norm, k_norm, rope, tm):
    nb, l, d = x.shape
    use_rope = rope is not None
    row = pl.BlockSpec((1, tm, d), lambda b, i: (b, i, 0))
    per_b = pl.BlockSpec((1, 1, d), lambda b, i: (b, 0, 0))

    def full(a):
        return pl.BlockSpec(a.shape, lambda b, i: (0,) * a.ndim)

    ins = [x, shift, scale, pre_g, w["qkv"], w["z"], w["q"], w["k"], w["vt"], w["g"], w["bat"], q_norm, k_norm]
    in_specs = [row, per_b, per_b] + [full(a) for a in ins[3:]]
    if use_rope:
        ins += list(rope)
        in_specs += [pl.BlockSpec((tm, HEAD_DIM), lambda b, i: (i, 0))] * 2
    n_qkv, n_z, n_q, n_k = w["qkv"].shape[1], w["z"].shape[1], w["q"].shape[1], w["k"].shape[1]
    n_vt, n_g, n_ba = w["vt"].shape[0], w["g"].shape[1], w["bat"].shape[0]

    def tok(n):
        return pl.BlockSpec((1, tm, n), lambda b, i: (b, i, 0))

    def chan(n):
        return pl.BlockSpec((1, n, tm), lambda b, i: (b, 0, i))

    out_shape = [jax.ShapeDtypeStruct((nb, l, n_qkv), BF16), jax.ShapeDtypeStruct((nb, l, n_z), BF16),
                 jax.ShapeDtypeStruct((nb, l, n_q), BF16), jax.ShapeDtypeStruct((nb, l, n_k), BF16),
                 jax.ShapeDtypeStruct((nb, n_vt, l), BF16), jax.ShapeDtypeStruct((nb, l, n_g), BF16),
                 jax.ShapeDtypeStruct((nb, n_ba, l), F32)]
    out_specs = [tok(n_qkv), tok(n_z), tok(n_q), tok(n_k), chan(n_vt), tok(n_g), chan(n_ba)]
    return pl.pallas_call(
        functools.partial(_inproj_kernel, use_rope=use_rope),
        grid=(nb, l // tm),
        in_specs=in_specs,
        out_specs=out_specs,
        out_shape=out_shape,
        compiler_params=_params("parallel", "parallel"),
        name="mixer_inproj",
    )(*ins)


def _softplus(x):
    return jnp.maximum(x, 0.0) + jnp.log1p(jnp.exp(-jnp.abs(x)))


def _split3(x):
    x1 = x.astype(BF16)
    r1 = x - x1.astype(F32)
    x2 = r1.astype(BF16)
    x3 = (r1 - x2.astype(F32)).astype(BF16)
    return x1, x2, x3


def _lane_rep(row):
    return jnp.transpose(jnp.broadcast_to(row, (HEAD_DIM, DN_WIN)))


def _dn_kernel(alog_ref, dtb_ref, qx_ref, kx_ref, vx_ref, qc_ref, kc_ref, vc_ref,
               cwq_ref, cwk_ref, cwv_ref, rows_ref, o_ref,
               stg, qs, ks, vs, oacc, rq, masks, qe_s, o0_s, dd_s, pp_s, rr_s, *, lc, l, hp):
    z = lc + l
    n_win = z // DN_WIN
    n_cwin = lc // DN_WIN
    hd0 = pl.program_id(1) * hp

    ri = lax.broadcasted_iota(jnp.int32, (DN_WIN, DN_WIN), 0)
    ci = lax.broadcasted_iota(jnp.int32, (DN_WIN, DN_WIN), 1)
    same = (ri // DN_CHUNK) == (ci // DN_CHUNK)
    masks[0] = (same & (ri >= ci)).astype(F32)
    masks[1] = (same & (ri > ci)).astype(F32)
    masks[2] = (same & (ri <= ci)).astype(F32)
    masks[3] = (same & (ri < ci)).astype(F32)
    masks[4] = (ri == ci).astype(F32)

    for hh in range(hp):
        neg_a = [-jnp.exp(jnp.full((1, DN_WIN), alog_ref[dr, hd0 + hh], F32)) for dr in range(2)]
        dtb = [jnp.full((1, DN_WIN), dtb_ref[dr, hd0 + hh], F32) for dr in range(2)]
        for w in range(n_win):
            r = rows_ref[0, hh, :, w * DN_WIN:(w + 1) * DN_WIN]
            beta = [jax.nn.sigmoid(r[dr:dr + 1]) for dr in range(2)]
            g = [neg_a[dr] * _softplus(r[2 + dr:3 + dr] + dtb[dr]) for dr in range(2)]
            parts = _split3(g[0]) + _split3(g[1])
            lhs = jnp.concatenate(list(parts) + [jnp.zeros((10, DN_WIN), BF16)], axis=0)
            pre = _dot(lhs, masks[2].astype(BF16))
            suf = _dot(lhs, masks[0].astype(BF16))
            pre = [pre[0:1] + pre[1:2] + pre[2:3], pre[3:4] + pre[4:5] + pre[5:6]]
            suf = [suf[0:1] + suf[1:2] + suf[2:3], suf[3:4] + suf[4:5] + suf[5:6]]
            gcum = [pre[0], suf[1]]
            rest = [suf[0] - g[0], pre[1] - g[1]]
            rows = []
            for dr in range(2):
                gam = jnp.exp(gcum[dr])
                rows += [beta[dr], gcum[dr], gam, beta[dr] * gam, jnp.exp(rest[dr]),
                         jnp.exp(gcum[dr] + rest[dr])]
            rows.append(jnp.zeros((4, DN_WIN), F32))
            rq[hh, w] = jnp.concatenate(rows, axis=0)

    zeros8 = jnp.zeros((8, HEAD_DIM), F32)
    tpos = lax.broadcasted_iota(jnp.int32, (DN_WIN, 1), 0)

    def conv_into(dst, hh, c_ref, x_ref, cw_ref, norm_scale):
        cols = slice(hh * HEAD_DIM, (hh + 1) * HEAD_DIM)
        stg[0:8, :] = zeros8
        stg[8:8 + lc, :] = c_ref[0, :, cols].astype(F32)
        stg[8 + lc:8 + z, :] = x_ref[0, :, cols].astype(F32)
        stg[8 + z:16 + z, :] = zeros8
        cw = cw_ref[:, cols]

        def blk(i, carry):
            r0 = pl.multiple_of(i * DN_WIN, DN_WIN)
            win = stg[pl.ds(r0, DN_WIN + 16), :]
            t = tpos + r0
            prev = jnp.where((t == 0) | (t == lc), 0.0, win[7:7 + DN_WIN])
            nxt = jnp.where((t == lc - 1) | (t == z - 1), 0.0, win[9:9 + DN_WIN])
            y = prev * cw[0:1] + win[8:8 + DN_WIN] * cw[1:2] + nxt * cw[2:3]
            y = jax.nn.silu(y)
            if norm_scale is not None:
                y = y * lax.rsqrt(jnp.sum(y * y, axis=-1, keepdims=True) + EPS)
                if norm_scale != 1.0:
                    y = y * norm_scale
            dst[hh, pl.ds(r0, DN_WIN), :] = y
            return carry

        lax.fori_loop(0, n_win, blk, 0)

    for hh in range(hp):
        conv_into(qs, hh, qc_ref, qx_ref, cwq_ref, HEAD_DIM ** -0.5)
        conv_into(ks, hh, kc_ref, kx_ref, cwk_ref, 1.0)
        conv_into(vs, hh, vc_ref, vx_ref, cwv_ref, None)

    oacc[...] = jnp.zeros(oacc.shape, F32)

    lane_chunk = lax.broadcasted_iota(jnp.int32, (1, DN_WIN), 1) // DN_CHUNK
    streams = [(hh, dr) for hh in range(hp) for dr in range(2)]

    def bwd_window(i):
        return jnp.where(i < n_cwin, n_cwin - 1 - i, n_win - 1 - (i - n_cwin))

    zero_slot = 1
    qe_s[zero_slot] = jnp.zeros(qe_s.shape[1:], BF16)
    o0_s[zero_slot] = jnp.zeros(o0_s.shape[1:], F32)
    dd_s[zero_slot] = jnp.zeros(dd_s.shape[1:], F32)
    pp_s[zero_slot] = jnp.zeros(pp_s.shape[1:], BF16)
    rr_s[zero_slot] = jnp.zeros(rr_s.shape[1:], F32)

    def step(i, carry):
        state = list(carry)
        cur = jnp.minimum(i, n_win - 1)
        prev = jnp.maximum(i - 1, 0)
        slot, pslot = i % 2, (i + 1) % 2
        wins = (cur, bwd_window(cur))
        r0 = [pl.multiple_of(w * DN_WIN, DN_WIN) for w in wins]
        pr0 = [pl.multiple_of(w * DN_WIN, DN_WIN) for w in (prev, bwd_window(prev))]

        def recur_step(k):
            cs = (k, CHUNKS_PER_WIN - 1 - k)
            blk = DN_CHUNK + HEAD_DIM
            for hh in range(hp):
                lhs = []
                for dr in range(2):
                    lo = cs[dr] * DN_CHUNK
                    lhs += [qe_s[pslot, hh, dr, lo:lo + DN_CHUNK, :], pp_s[pslot, hh, dr, cs[dr]]]
                s_pair = state[2 * hh:2 * hh + 2]
                res = _dot(jnp.concatenate(lhs, axis=0), jnp.concatenate(s_pair, axis=1).astype(BF16))
                for dr in range(2):
                    lo = cs[dr] * DN_CHUNK
                    sub = res[dr * blk:(dr + 1) * blk, dr * HEAD_DIM:(dr + 1) * HEAD_DIM]
                    rows = pl.ds(pr0[dr] + lo, DN_CHUNK)
                    oacc[hh, rows, :] = (oacc[hh, rows, :] + o0_s[pslot, hh, dr, lo:lo + DN_CHUNK, :]
                                         + sub[:DN_CHUNK])
                    dv = dd_s[pslot, hh, dr, lo:lo + DN_CHUNK, :]
                    state[2 * hh + dr] = (jnp.concatenate([dv, dv], axis=0) * s_pair[dr] + sub[DN_CHUNK:]
                                          + rr_s[pslot, hh, dr, cs[dr]])

        blkx = (lax.broadcasted_iota(jnp.int32, (DN_WIN, DN_WIN), 0)
                ^ lax.broadcasted_iota(jnp.int32, (DN_WIN, DN_WIN), 1))
        st = []
        for hh, dr in streams:
            rqw = rq[hh, wins[dr]]
            b0 = 6 * dr
            st.append(dict(
                hh=hh, dr=dr, rqw=rqw, b0=b0,
                kw=ks[hh, pl.ds(r0[dr], DN_WIN), :],
                qw=qs[hh, pl.ds(r0[dr], DN_WIN), :],
                vw=vs[hh, pl.ds(r0[dr], DN_WIN), :],
                beta=_lane_rep(rqw[b0:b0 + 1]),
                gcol=_lane_rep(rqw[b0 + 1:b0 + 2])))
        for x in st:
            kb = x["kw"].astype(BF16)
            x["kk"] = _dot_nt(kb, kb)
            x["qk"] = _dot_nt(x["qw"].astype(BF16), kb)
        recur_step(0)
        for x in st:
            b0, rqw = x["b0"], x["rqw"]
            incl = masks[0] if x["dr"] == 0 else masks[2]
            strict = masks[1] if x["dr"] == 0 else masks[3]
            gdiff = jnp.concatenate([x["gcol"]] * 2, axis=1) - rqw[b0 + 1:b0 + 2]
            e = jnp.exp(jnp.minimum(gdiff, 0.0))
            x["am"] = (x["qk"] * e * incl).astype(BF16)
            x["lm"] = x["kk"] * e * jnp.concatenate([x["beta"]] * 2, axis=1) * strict
            x["t"] = masks[4] - jnp.where(blkx == 1, x["lm"], 0.0)
        for lvl in range(1, INVERSE_LEVELS):
            for x in st:
                nb_ = jnp.where((blkx >> lvl) == 1, x["lm"], 0.0).astype(BF16)
                x["y"] = _dot(nb_, x["t"].astype(BF16)).astype(BF16)
            for x in st:
                x["t"] = x["t"] - _dot(x["t"].astype(BF16), x["y"])
            if lvl % 2 == 1 and 1 + lvl // 2 < CHUNKS_PER_WIN:
                recur_step(1 + lvl // 2)
        for x in st:
            b0, rqw = x["b0"], x["rqw"]
            bgam = _lane_rep(rqw[b0 + 3:b0 + 4])
            rhs = jnp.concatenate([x["vw"] * x["beta"], x["kw"] * bgam], axis=1).astype(BF16)
            x["uwb"] = _dot(x["t"].astype(BF16), rhs).astype(BF16)
        for x in st:
            hh, dr, b0, rqw = x["hh"], x["dr"], x["b0"], x["rqw"]
            auw = _dot(x["am"], x["uwb"])
            o0_s[slot, hh, dr] = auw[:, :HEAD_DIM]
            qe_s[slot, hh, dr] = (x["qw"] * _lane_rep(rqw[b0 + 2:b0 + 3]) - auw[:, HEAD_DIM:]).astype(BF16)
            dd_s[slot, hh, dr] = _lane_rep(rqw[b0 + 5:b0 + 6])
            x["kdt"] = jnp.transpose(x["kw"] * _lane_rep(rqw[b0 + 4:b0 + 5]))
        for c in range(CHUNKS_PER_WIN):
            for x in st:
                kdt_c = jnp.where(lane_chunk == c, x["kdt"], 0.0).astype(BF16)
                pr = _dot(kdt_c, x["uwb"])
                rr_s[slot, x["hh"], x["dr"], c] = pr[:, :HEAD_DIM]
                pp_s[slot, x["hh"], x["dr"], c] = (-pr[:, HEAD_DIM:]).astype(BF16)
        return tuple(state)

    s0 = jnp.zeros((HEAD_DIM, HEAD_DIM), F32)
    lax.fori_loop(0, n_win + 1, step, (s0,) * len(streams))

    for hh in range(hp):
        o_ref[0, :, hh * HEAD_DIM:(hh + 1) * HEAD_DIM] = oacc[hh, lc:z, :].astype(BF16)


def _deltanet(qkv_x, qkv_c, conv_w, rows, a_log, dt_bias):
    nb, l, _ = qkv_x.shape
    lc = qkv_c.shape[1]
    z = lc + l
    n_win = z // DN_WIN
    hp = DN_HEADS_PER_STEP
    ng = DN_HEADS // hp
    gw = hp * HEAD_DIM

    def col(seq, off):
        return pl.BlockSpec((1, seq, gw), lambda b, h: (b, 0, off + h))

    def cw(off):
        return pl.BlockSpec((conv_w.shape[0], gw), lambda b, h: (0, off + h))

    smem = pl.BlockSpec(memory_space=pltpu.SMEM)
    slot_shape = (2, hp, 2)
    return pl.pallas_call(
        functools.partial(_dn_kernel, lc=lc, l=l, hp=hp),
        grid=(nb, ng),
        in_specs=[smem, smem,
                  col(l, 0), col(l, ng), col(l, 2 * ng), col(lc, 0), col(lc, ng), col(lc, 2 * ng),
                  cw(0), cw(ng), cw(2 * ng),
                  pl.BlockSpec((1, hp, 8, z), lambda b, h: (b, h, 0, 0))],
        out_specs=pl.BlockSpec((1, l, gw), lambda b, h: (b, 0, h)),
        out_shape=jax.ShapeDtypeStruct((nb, l, DN_HEADS * HEAD_DIM), BF16),
        scratch_shapes=[pltpu.VMEM((z + 16, HEAD_DIM), F32),
                        pltpu.VMEM((hp, z, HEAD_DIM), F32),
                        pltpu.VMEM((hp, z, HEAD_DIM), F32),
                        pltpu.VMEM((hp, z, HEAD_DIM), F32),
                        pltpu.VMEM((hp, z, HEAD_DIM), F32),
                        pltpu.VMEM((hp, n_win, 16, DN_WIN), F32),
                        pltpu.VMEM((5, DN_WIN, DN_WIN), F32),
                        pltpu.VMEM(slot_shape + (DN_WIN, HEAD_DIM), BF16),
                        pltpu.VMEM(slot_shape + (DN_WIN, HEAD_DIM), F32),
                        pltpu.VMEM(slot_shape + (DN_WIN, HEAD_DIM), F32),
                        pltpu.VMEM(slot_shape + (CHUNKS_PER_WIN, HEAD_DIM, HEAD_DIM), BF16),
                        pltpu.VMEM(slot_shape + (CHUNKS_PER_WIN, HEAD_DIM, HEAD_DIM), F32)],
        compiler_params=_params("parallel", "parallel"),
        name="deltanet",
    )(a_log, dt_bias, qkv_x, qkv_x, qkv_x, qkv_c, qkv_c, qkv_c, conv_w, conv_w, conv_w, rows)


def _attn_kernel(q_ref, k_ref, vt_ref, o_ref, *, tq, tk, n_kblk):
    q = q_ref[0]
    qm = jnp.concatenate([q[:, g * HEAD_DIM:(g + 1) * HEAD_DIM] for g in range(ATTN_GROUP)], axis=0)
    ncol = ATTN_GROUP * tq

    def scores(j):
        return _dot_nt(k_ref[0, j * tk:(j + 1) * tk, :], qm)

    def add_values(acc, pend):
        j, alpha, pb = pend
        return alpha * acc + _dot(vt_ref[0, 0, :, j * tk:(j + 1) * tk], pb)

    m = jnp.full((1, ncol), -jnp.inf, F32)
    acc = jnp.zeros((HEAD_DIM + ATTN_ONES_ROWS, ncol), F32)
    ahead = [scores(j) for j in range(min(ATTN_SCORE_LOOKAHEAD, n_kblk))]
    pend = None
    for j in range(n_kblk):
        s = ahead.pop(0)
        if j + ATTN_SCORE_LOOKAHEAD < n_kblk:
            ahead.append(scores(j + ATTN_SCORE_LOOKAHEAD))
        if pend is not None:
            acc = add_values(acc, pend)
        m_new = jnp.maximum(m, jnp.max(s, axis=0, keepdims=True))
        pend = (j, jnp.exp2(m - m_new), jnp.exp2(s - m_new).astype(BF16))
        m = m_new
    acc = add_values(acc, pend)
    ot = acc[:HEAD_DIM] / acc[HEAD_DIM:HEAD_DIM + 1]
    for g in range(ATTN_GROUP):
        o_ref[0, :, g * HEAD_DIM:(g + 1) * HEAD_DIM] = jnp.transpose(ot[:, g * tq:(g + 1) * tq]).astype(BF16)


def _attention(q, k_all, vt_all, tq, tk):
    nb, l, _ = q.shape
    z = k_all.shape[1]
    gw = ATTN_GROUP * HEAD_DIM
    return pl.pallas_call(
        functools.partial(_attn_kernel, tq=tq, tk=tk, n_kblk=z // tk),
        grid=(nb, ATTN_KV_HEADS, l // tq),
        in_specs=[pl.BlockSpec((1, tq, gw), lambda b, kv, i: (b, i, kv)),
                  pl.BlockSpec((1, z, HEAD_DIM), lambda b, kv, i: (b, 0, kv)),
                  pl.BlockSpec((1, 1, HEAD_DIM + ATTN_ONES_ROWS, z), lambda b, kv, i: (b, kv, 0, 0))],
        out_specs=pl.BlockSpec((1, tq, gw), lambda b, kv, i: (b, i, kv)),
        out_shape=jax.ShapeDtypeStruct((nb, l, ATTN_HEADS * HEAD_DIM), BF16),
        compiler_params=_params("parallel", "parallel", "parallel"),
        name="gqa_attn",
    )(q, k_all, vt_all)


def _merge_kernel(x_ref, gate_ref, odn_ref, z_ref, oat_ref, g_ref, dnn_ref, post_ref,
                  wdn_ref, wat_ref, wo_ref, o_ref):
    x = x_ref[0]
    d = x.shape[-1]
    odn = odn_ref[0].astype(F32)
    zz = z_ref[0].astype(F32)
    parts = []
    for hd in range(DN_HEADS):
        sl = slice(hd * HEAD_DIM, (hd + 1) * HEAD_DIM)
        parts.append((_rms(odn[:, sl], dnn_ref[...]) * jax.nn.silu(zz[:, sl])).astype(BF16))
    y_dn = _dot(jnp.concatenate(parts, axis=1), wdn_ref[...])
    y_at = _dot(oat_ref[0], wat_ref[...])
    g = g_ref[0].astype(F32)
    y = jax.nn.sigmoid(g[:, :d]) * y_dn + jax.nn.sigmoid(g[:, d:]) * y_at
    yy = _dot(y.astype(BF16), wo_ref[...])
    o_ref[0] = x + gate_ref[0] * _rms(yy, post_ref[...])


def _merge(x, gate, o_dn, z, o_at, gates, dn_norm, post_g, w_dn, w_at, w_o, tm):
    nb, l, d = x.shape

    def tok(n):
        return pl.BlockSpec((1, tm, n), lambda b, i: (b, i, 0))

    def full(a):
        return pl.BlockSpec(a.shape, lambda b, i: (0,) * a.ndim)

    return pl.pallas_call(
        _merge_kernel,
        grid=(nb, l // tm),
        in_specs=[tok(d), pl.BlockSpec((1, 1, d), lambda b, i: (b, 0, 0)), tok(o_dn.shape[-1]), tok(z.shape[-1]),
                  tok(o_at.shape[-1]), tok(gates.shape[-1]), full(dn_norm), full(post_g),
                  full(w_dn), full(w_at), full(w_o)],
        out_specs=tok(d),
        out_shape=jax.ShapeDtypeStruct((nb, l, d), F32),
        compiler_params=_params("parallel", "parallel"),
        name="mixer_merge",
    )(x, gate, o_dn, z, o_at, gates, dn_norm, post_g, w_dn, w_at, w_o)


def _rope_tables(l):
    rows = l // GRID_W
    row = jnp.repeat(jnp.arange(rows, dtype=jnp.int32), GRID_W)
    colp = jnp.tile(jnp.arange(GRID_W, dtype=jnp.int32), rows)
    pos = jnp.stack([row, colp], axis=-1).astype(F32)
    n_freq = HEAD_DIM // 4
    inv_freq = ROPE_THETA ** (-jnp.arange(n_freq, dtype=F32) / n_freq)
    ang = pos[:, :, None] * inv_freq
    cos, sin = jnp.cos(ang), jnp.sin(ang)
    cosf = jnp.concatenate([cos[:, 0], cos[:, 0], cos[:, 1], cos[:, 1]], axis=-1)
    sinf = jnp.concatenate([-sin[:, 0], sin[:, 0], -sin[:, 1], sin[:, 1]], axis=-1)
    return cosf, sinf


def _split_w_in(w_in):
    d = w_in.shape[0]
    dn_qkv = 3 * DN_HEADS * HEAD_DIM
    dn_v = DN_HEADS * HEAD_DIM
    n_ba = 4 * DN_HEADS
    a_q = ATTN_HEADS * HEAD_DIM
    a_kv = ATTN_KV_HEADS * HEAD_DIM
    sizes = (dn_qkv, dn_v, n_ba, a_q, a_kv, a_kv, 2 * d)
    offs = [0]
    for sz in sizes:
        offs.append(offs[-1] + sz)
    assert offs[-1] == w_in.shape[1]
    wb = w_in.astype(BF16)
    part = [wb[:, offs[i]:offs[i + 1]] for i in range(len(sizes))]
    return {"qkv": part[0], "z": part[1], "bat": part[2].T, "q": part[3], "k": part[4],
            "vt": part[5].T, "g": part[6]}


def kernel(x, c, ctx, c_ctx, w_mod, b_mod, ffn1_pre, ffn1_post, ffn1_w_gu, ffn1_w_down, mix_pre, mix_post, w_in, dn_conv, dn_a_log, dn_dt_bias, dn_out_norm, w_dn_out, attn_q_norm, attn_k_norm, w_attn_out, w_out, ffn2_pre, ffn2_post, ffn2_w_gu, ffn2_w_down):
    nb, l, d = x.shape
    lc = ctx.shape[1]
    assert w_mod.shape[0] == 1, "single-layer block: the context stream is never updated"
    assert l % DN_WIN == 0 and lc % DN_WIN == 0 and l % GRID_W == 0
    tm_x = min(512, l)
    tm_c = min(256, lc)

    pad = (-(nb + 1)) % 8
    cc = jnp.concatenate([c, c_ctx[None, :], jnp.zeros((pad, d), F32)], axis=0)
    mod_all = _modulation(cc, w_mod[0], b_mod[0])
    mod = [mod_all[:nb, j * d:(j + 1) * d][:, None, :] for j in range(N_MOD)]
    mod_c = [jnp.broadcast_to(mod_all[nb:nb + 1, j * d:(j + 1) * d][:, None, :], (nb, 1, d)) for j in range(N_MOD)]

    f1 = _ffn_weights(ffn1_w_gu[0], ffn1_w_down[0])
    x1 = _ffn(x, mod[0], mod[1], mod[2], ffn1_pre, ffn1_post, *f1, tm=tm_x)
    c1 = _ffn(ctx, mod_c[0], mod_c[1], mod_c[2], ffn1_pre, ffn1_post, *f1, tm=tm_c)

    w = _split_w_in(w_in[0])
    rope = _rope_tables(l)
    qkv_x, z_x, qa_x, ka_x, vt_x, g_x, bat_x = _inproj(
        x1, mod[3], mod[4], mix_pre, w, attn_q_norm, attn_k_norm, rope, tm=min(256, l))
    qkv_c, _, _, ka_c, vt_c, _, bat_c = _inproj(
        c1, mod_c[3], mod_c[4], mix_pre, w, attn_q_norm, attn_k_norm, None, tm=tm_c)

    z = lc + l
    bat = jnp.concatenate([bat_c, bat_x], axis=-1)
    rows = bat.reshape(nb, 2, 2, DN_HEADS, z).transpose(0, 3, 1, 2, 4).reshape(nb, DN_HEADS, 4, z)
    rows = jnp.concatenate([rows, jnp.zeros_like(rows)], axis=2)
    o_dn = _deltanet(qkv_x, qkv_c, dn_conv[0], rows, dn_a_log[0], dn_dt_bias[0])

    k_all = jnp.concatenate([ka_c, ka_x], axis=1)
    vt_all = jnp.concatenate([vt_c, vt_x], axis=2).reshape(nb, ATTN_KV_HEADS, HEAD_DIM, z)
    vt_all = jnp.concatenate([vt_all, jnp.ones((nb, ATTN_KV_HEADS, ATTN_ONES_ROWS, z), BF16)], axis=2)
    o_at = _attention(qa_x, k_all, vt_all, tq=min(256, l), tk=DN_WIN)

    x2 = _merge(x1, mod[5], o_dn, z_x, o_at, g_x, dn_out_norm, mix_post,
                w_dn_out[0].astype(BF16), w_attn_out[0].astype(BF16), w_out[0].astype(BF16), tm=tm_x)

    f2 = _ffn_weights(ffn2_w_gu[0], ffn2_w_down[0])
    return _ffn(x2, mod[6], mod[7], mod[8], ffn2_pre, ffn2_post, *f2, tm=tm_x)
```

```python
import functools

import jax
import jax.numpy as jnp
from jax import lax
from jax.experimental import pallas as pl
from jax.experimental.pallas import tpu as pltpu

F32 = jnp.float32
BF16 = jnp.bfloat16

EPS = 1e-6
GRID_W = 64
ROPE_THETA = 10000.0
N_MOD = 9
FFN_RES_WEIGHT = 0.5

HEAD_DIM = 128
DN_HEADS = 8
ATTN_HEADS = 8
ATTN_KV_HEADS = 2
ATTN_GROUP = ATTN_HEADS // ATTN_KV_HEADS
DN_CHUNK = 64
DN_WIN = 256
CHUNKS_PER_WIN = DN_WIN // DN_CHUNK
DN_HEADS_PER_STEP = 4
INVERSE_LEVELS = 6
FFN_HIDDEN_CHUNK = 256
CONV_HALO = 8
ATTN_SCORE_LOOKAHEAD = 2
ATTN_ONES_ROWS = 16
ATTN_Q_SCALE = HEAD_DIM ** -0.5 * 1.4426950408889634

VMEM_LIMIT_BYTES = 56 * 1024 * 1024


def _dot(a, b):
    return jnp.dot(a, b, preferred_element_type=F32)


def _dot_nt(a, b):
    return lax.dot_general(a, b, (((1,), (1,)), ((), ())), preferred_element_type=F32)


def _rms(x, g):
    return x * lax.rsqrt(jnp.mean(x * x, axis=-1, keepdims=True) + EPS) * g


def _params(*sem):
    return pltpu.CompilerParams(dimension_semantics=sem, vmem_limit_bytes=VMEM_LIMIT_BYTES)


def _mod_kernel(c_ref, w_ref, b_ref, o_ref):
    a = jax.nn.silu(c_ref[...])
    o_ref[...] = jnp.dot(a, w_ref[...], preferred_element_type=F32,
                         precision=lax.Precision.HIGHEST) + b_ref[...]


def _modulation(cc, w_mod, b_mod):
    rows, d = cc.shape
    n = w_mod.shape[1]
    tn = d
    return pl.pallas_call(
        _mod_kernel,
        grid=(n // tn,),
        in_specs=[pl.BlockSpec((rows, d), lambda j: (0, 0)),
                  pl.BlockSpec((d, tn), lambda j: (0, j)),
                  pl.BlockSpec((1, tn), lambda j: (0, j))],
        out_specs=pl.BlockSpec((rows, tn), lambda j: (0, j)),
        out_shape=jax.ShapeDtypeStruct((rows, n), F32),
        compiler_params=_params("arbitrary"),
        name="adaln_mod",
    )(cc, w_mod, b_mod.reshape(1, n))


def _ffn_kernel(x_ref, sh_ref, sc_ref, gt_ref, pre_ref, post_ref, wg_ref, wu_ref, wd_ref, o_ref, *, n_chunks):
    x = x_ref[0]
    h = _rms(x, pre_ref[...]) * (1.0 + sc_ref[0]) + sh_ref[0]
    hb = h.astype(BF16)
    acc = None
    for j in range(n_chunks):
        a = _dot(hb, wg_ref[j])
        b = _dot(hb, wu_ref[j])
        act = (jax.nn.silu(a) * b).astype(BF16)
        d = _dot(act, wd_ref[j])
        acc = d if acc is None else acc + d
    o_ref[0] = x + FFN_RES_WEIGHT * gt_ref[0] * _rms(acc, post_ref[...])


def _ffn(x, shift, scale, gate, pre_g, post_g, wg, wu, wd, tm):
    nb, l, d = x.shape
    n_chunks, _, hc = wg.shape
    row = pl.BlockSpec((1, tm, d), lambda b, i: (b, i, 0))
    per_b = pl.BlockSpec((1, 1, d), lambda b, i: (b, 0, 0))
    vec = pl.BlockSpec((1, d), lambda b, i: (0, 0))
    w_up = pl.BlockSpec((n_chunks, d, hc), lambda b, i: (0, 0, 0))
    w_dn = pl.BlockSpec((n_chunks, hc, d), lambda b, i: (0, 0, 0))
    return pl.pallas_call(
        functools.partial(_ffn_kernel, n_chunks=n_chunks),
        grid=(nb, l // tm),
        in_specs=[row, per_b, per_b, per_b, vec, vec, w_up, w_up, w_dn],
        out_specs=row,
        out_shape=jax.ShapeDtypeStruct((nb, l, d), F32),
        compiler_params=_params("parallel", "parallel"),
        name="ffn",
    )(x, shift, scale, gate, pre_g, post_g, wg, wu, wd)


def _ffn_weights(w_gu, w_down):
    d, two_h = w_gu.shape
    hid = two_h // 2
    n_chunks = hid // FFN_HIDDEN_CHUNK
    wgu = w_gu.astype(BF16)
    wg = wgu[:, :hid].reshape(d, n_chunks, FFN_HIDDEN_CHUNK).transpose(1, 0, 2)
    wu = wgu[:, hid:].reshape(d, n_chunks, FFN_HIDDEN_CHUNK).transpose(1, 0, 2)
    wd = w_down.astype(BF16).reshape(n_chunks, FFN_HIDDEN_CHUNK, d)
    return wg, wu, wd


def _head_norm_rope(r, gain, n_heads, scale, rope):
    out = []
    if rope is not None:
        cosf, sinf = rope
        lane = lax.broadcasted_iota(jnp.int32, (1, HEAD_DIM), 1)
        first_half = (lane % (HEAD_DIM // 2)) < (HEAD_DIM // 4)
    for hd in range(n_heads):
        xh = r[:, hd * HEAD_DIM:(hd + 1) * HEAD_DIM]
        xh = _rms(xh, gain)
        if rope is not None:
            up = pltpu.roll(xh, HEAD_DIM - HEAD_DIM // 4, axis=1)
            dn = pltpu.roll(xh, HEAD_DIM // 4, axis=1)
            xh = xh * cosf + jnp.where(first_half, up, dn) * sinf
        if scale != 1.0:
            xh = xh * scale
        out.append(xh)
    return out


def _inproj_kernel(*refs, use_rope, seq_len):
    if use_rope:
        (x_ref, xp_ref, xn_ref, sh_ref, sc_ref, pre_ref, wqkv_ref, wz_ref, wq_ref, wk_ref, wvt_ref, wg_ref,
         wbat_ref, qn_ref, kn_ref, cw_ref, cos_ref, sin_ref, oqkv, oz, oq, ok, ovt, og, obat) = refs
        rope = (cos_ref[...], sin_ref[...])
    else:
        (x_ref, xp_ref, xn_ref, sh_ref, sc_ref, pre_ref, wqkv_ref, wz_ref, wq_ref, wk_ref, wvt_ref, wg_ref,
         wbat_ref, qn_ref, kn_ref, cw_ref, oqkv, oz, oq, ok, ovt, og, obat) = refs
        rope = None
    tm, d = x_ref.shape[1], x_ref.shape[2]
    xe = jnp.concatenate([xp_ref[0], x_ref[0], xn_ref[0]], axis=0)
    he = _rms(xe, pre_ref[...]) * (1.0 + sc_ref[0]) + sh_ref[0]
    hbe = he.astype(BF16)
    hb = he[CONV_HALO:CONV_HALO + tm].astype(BF16)

    tpos = pl.program_id(1) * tm + lax.broadcasted_iota(jnp.int32, (tm, 1), 0)

    def dn_epilogue(j):
        def run(pre):
            cw = cw_ref[:, j * d:(j + 1) * d]
            prev = jnp.where(tpos == 0, 0.0, pre[CONV_HALO - 1:CONV_HALO - 1 + tm])
            nxt = jnp.where(tpos == seq_len - 1, 0.0, pre[CONV_HALO + 1:CONV_HALO + 1 + tm])
            y = jax.nn.silu(prev * cw[0:1] + pre[CONV_HALO:CONV_HALO + tm] * cw[1:2] + nxt * cw[2:3])
            for hd in range(d // HEAD_DIM):
                yh = y[:, hd * HEAD_DIM:(hd + 1) * HEAD_DIM]
                if j < 2:
                    yh = yh * lax.rsqrt(jnp.sum(yh * yh, axis=-1, keepdims=True) + EPS)
                    if j == 0:
                        yh = yh * HEAD_DIM ** -0.5
                oqkv[0, :, j * d + hd * HEAD_DIM:j * d + (hd + 1) * HEAD_DIM] = yh.astype(BF16)
        return run

    def store_cast(o_ref, j):
        def run(r):
            o_ref[0, :, j * d:(j + 1) * d] = r.astype(BF16)
        return run

    def attn_epilogue(o_ref, gain_ref, n_heads, scale):
        def run(r):
            hs = _head_norm_rope(r, gain_ref[...], n_heads, scale, rope)
            for hd in range(n_heads):
                o_ref[0, :, hd * HEAD_DIM:(hd + 1) * HEAD_DIM] = hs[hd].astype(BF16)
        return run

    def store_vt(r):
        ovt[0] = r.astype(BF16)

    def store_bat(r):
        obat[0] = r

    stages = [(functools.partial(_dot, hbe, wqkv_ref[:, j * d:(j + 1) * d]), dn_epilogue(j))
              for j in range(wqkv_ref.shape[1] // d)]
    stages.append((lambda: _dot(hb, wq_ref[...]), attn_epilogue(oq, qn_ref, ATTN_HEADS, ATTN_Q_SCALE)))
    stages.append((lambda: _dot(hb, wk_ref[...]), attn_epilogue(ok, kn_ref, ATTN_KV_HEADS, 1.0)))
    stages.append((lambda: _dot(hb, wz_ref[...]), store_cast(oz, 0)))
    stages += [(functools.partial(_dot, hb, wg_ref[:, j * d:(j + 1) * d]), store_cast(og, j))
               for j in range(wg_ref.shape[1] // d)]
    stages.append((lambda: _dot_nt(wvt_ref[...], hb), store_vt))
    stages.append((lambda: _dot_nt(wbat_ref[...], hb), store_bat))
    result = stages[0][0]()
    for j, (_, epilogue) in enumerate(stages):
        upcoming = stages[j + 1][0]() if j + 1 < len(stages) else None
        epilogue(result)
        result = upcoming


def _inproj(x, shift, scale, pre_g, w, q_norm, k_norm, conv_w, rope, tm):
    nb, l, d = x.shape
    use_rope = rope is not None
    row = pl.BlockSpec((1, tm, d), lambda b, i: (b, i, 0))
    hpt = tm // CONV_HALO
    n_halo = l // CONV_HALO
    row_prev = pl.BlockSpec((1, CONV_HALO, d), lambda b, i: (b, jnp.maximum(i * hpt - 1, 0), 0))
    row_next = pl.BlockSpec((1, CONV_HALO, d), lambda b, i: (b, jnp.minimum((i + 1) * hpt, n_halo - 1), 0))
    per_b = pl.BlockSpec((1, 1, d), lambda b, i: (b, 0, 0))

    def full(a):
        return pl.BlockSpec(a.shape, lambda b, i: (0,) * a.ndim)

    ins = [x, x, x, shift, scale, pre_g, w["qkv"], w["z"], w["q"], w["k"], w["vt"], w["g"], w["bat"],
           q_norm, k_norm, conv_w]
    in_specs = [row, row_prev, row_next, per_b, per_b] + [full(a) for a in ins[5:]]
    if use_rope:
        ins += list(rope)
        in_specs += [pl.BlockSpec((tm, HEAD_DIM), lambda b, i: (i, 0))] * 2
    n_qkv, n_z, n_q, n_k = w["qkv"].shape[1], w["z"].shape[1], w["q"].shape[1], w["k"].shape[1]
    n_vt, n_g, n_ba = w["vt"].shape[0], w["g"].shape[1], w["bat"].shape[0]

    def tok(n):
        return pl.BlockSpec((1, tm, n), lambda b, i: (b, i, 0))

    def chan(n):
        return pl.BlockSpec((1, n, tm), lambda b, i: (b, 0, i))

    out_shape = [jax.ShapeDtypeStruct((nb, l, n_qkv), BF16), jax.ShapeDtypeStruct((nb, l, n_z), BF16),
                 jax.ShapeDtypeStruct((nb, l, n_q), BF16), jax.ShapeDtypeStruct((nb, l, n_k), BF16),
                 jax.ShapeDtypeStruct((nb, n_vt, l), BF16), jax.ShapeDtypeStruct((nb, l, n_g), BF16),
                 jax.ShapeDtypeStruct((nb, n_ba, l), F32)]
    out_specs = [tok(n_qkv), tok(n_z), tok(n_q), tok(n_k), chan(n_vt), tok(n_g), chan(n_ba)]
    return pl.pallas_call(
        functools.partial(_inproj_kernel, use_rope=use_rope, seq_len=l),
        grid=(nb, l // tm),
        in_specs=in_specs,
        out_specs=out_specs,
        out_shape=out_shape,
        compiler_params=_params("parallel", "parallel"),
        name="mixer_inproj",
    )(*ins)


def _softplus(x):
    return jnp.maximum(x, 0.0) + jnp.log1p(jnp.exp(-jnp.abs(x)))


def _split3(x):
    x1 = x.astype(BF16)
    r1 = x - x1.astype(F32)
    x2 = r1.astype(BF16)
    x3 = (r1 - x2.astype(F32)).astype(BF16)
    return x1, x2, x3


def _lane_rep(row):
    return jnp.transpose(jnp.broadcast_to(row, (HEAD_DIM, DN_WIN)))


def _take_rows(a, first, half):
    return jnp.concatenate([a[r:r + half] for r in range(first, a.shape[0], 2 * half)], axis=0)


def _spread_rows(p, first, half):
    zero = jnp.zeros((half, p.shape[1]), p.dtype)
    out = []
    for i in range(p.shape[0] // half):
        blk = p[i * half:(i + 1) * half]
        out += [zero, blk] if first else [blk, zero]
    return jnp.concatenate(out, axis=0)


def _dn_kernel(alog_ref, dtb_ref, qx_ref, kx_ref, vx_ref, qc_ref, kc_ref, vc_ref, rows_ref, o_ref,
               oacc, rq, masks, qe_s, o0_s, dd_s, pp_s, rr_s, *, lc, l, hp):
    z = lc + l
    n_win = z // DN_WIN
    n_cwin = lc // DN_WIN
    hd0 = pl.program_id(1) * hp

    ri = lax.broadcasted_iota(jnp.int32, (DN_WIN, DN_WIN), 0)
    ci = lax.broadcasted_iota(jnp.int32, (DN_WIN, DN_WIN), 1)
    same = (ri // DN_CHUNK) == (ci // DN_CHUNK)
    masks[0] = (same & (ri >= ci)).astype(F32)
    masks[1] = (same & (ri > ci)).astype(F32)
    masks[2] = (same & (ri <= ci)).astype(F32)
    masks[3] = (same & (ri < ci)).astype(F32)
    masks[4] = (ri == ci).astype(F32)

    for hh in range(hp):
        neg_a = [-jnp.exp(jnp.full((1, DN_WIN), alog_ref[dr, hd0 + hh], F32)) for dr in range(2)]
        dtb = [jnp.full((1, DN_WIN), dtb_ref[dr, hd0 + hh], F32) for dr in range(2)]
        for w in range(n_win):
            r = rows_ref[0, hh, :, w * DN_WIN:(w + 1) * DN_WIN]
            beta = [jax.nn.sigmoid(r[dr:dr + 1]) for dr in range(2)]
            g = [neg_a[dr] * _softplus(r[2 + dr:3 + dr] + dtb[dr]) for dr in range(2)]
            parts = _split3(g[0]) + _split3(g[1])
            lhs = jnp.concatenate(list(parts) + [jnp.zeros((10, DN_WIN), BF16)], axis=0)
            pre = _dot(lhs, masks[2].astype(BF16))
            suf = _dot(lhs, masks[0].astype(BF16))
            pre = [pre[0:1] + pre[1:2] + pre[2:3], pre[3:4] + pre[4:5] + pre[5:6]]
            suf = [suf[0:1] + suf[1:2] + suf[2:3], suf[3:4] + suf[4:5] + suf[5:6]]
            gcum = [pre[0], suf[1]]
            rest = [suf[0] - g[0], pre[1] - g[1]]
            rows = []
            for dr in range(2):
                gam = jnp.exp(gcum[dr])
                rows += [beta[dr], gcum[dr], gam, beta[dr] * gam, jnp.exp(rest[dr]),
                         jnp.exp(gcum[dr] + rest[dr])]
            rows.append(jnp.zeros((4, DN_WIN), F32))
            rq[hh, w] = jnp.concatenate(rows, axis=0)

    oacc[...] = jnp.zeros(oacc.shape, F32)

    lane_chunk = lax.broadcasted_iota(jnp.int32, (1, DN_WIN), 1) // DN_CHUNK
    streams = [(hh, dr) for hh in range(hp) for dr in range(2)]

    def bwd_window(i):
        return jnp.where(i < n_cwin, n_cwin - 1 - i, n_win - 1 - (i - n_cwin))

    zero_slot = 1
    qe_s[zero_slot] = jnp.zeros(qe_s.shape[1:], BF16)
    o0_s[zero_slot] = jnp.zeros(o0_s.shape[1:], F32)
    dd_s[zero_slot] = jnp.zeros(dd_s.shape[1:], F32)
    pp_s[zero_slot] = jnp.zeros(pp_s.shape[1:], BF16)
    rr_s[zero_slot] = jnp.zeros(rr_s.shape[1:], F32)

    def step(i, carry):
        state = list(carry)
        cur = jnp.minimum(i, n_win - 1)
        prev = jnp.maximum(i - 1, 0)
        slot, pslot = i % 2, (i + 1) % 2
        wins = (cur, bwd_window(cur))
        pr0 = [pl.multiple_of(w * DN_WIN, DN_WIN) for w in (prev, bwd_window(prev))]

        def recur_step(k):
            cs = (k, CHUNKS_PER_WIN - 1 - k)
            blk = DN_CHUNK + HEAD_DIM
            for hh in range(hp):
                lhs = []
                for dr in range(2):
                    lo = cs[dr] * DN_CHUNK
                    lhs += [qe_s[pslot, hh, dr, lo:lo + DN_CHUNK, :], pp_s[pslot, hh, dr, cs[dr]]]
                s_pair = state[2 * hh:2 * hh + 2]
                res = _dot(jnp.concatenate(lhs, axis=0), jnp.concatenate(s_pair, axis=1).astype(BF16))
                for dr in range(2):
                    lo = cs[dr] * DN_CHUNK
                    sub = res[dr * blk:(dr + 1) * blk, dr * HEAD_DIM:(dr + 1) * HEAD_DIM]
                    rows = pl.ds(pr0[dr] + lo, DN_CHUNK)
                    oacc[hh, rows, :] = (oacc[hh, rows, :] + o0_s[pslot, hh, dr, lo:lo + DN_CHUNK, :]
                                         + sub[:DN_CHUNK])
                    dv = dd_s[pslot, hh, dr, lo:lo + DN_CHUNK, :]
                    state[2 * hh + dr] = (jnp.concatenate([dv, dv], axis=0) * s_pair[dr] + sub[DN_CHUNK:]
                                          + rr_s[pslot, hh, dr, cs[dr]])

        blkx = (lax.broadcasted_iota(jnp.int32, (DN_WIN, DN_WIN), 0)
                ^ lax.broadcasted_iota(jnp.int32, (DN_WIN, DN_WIN), 1))

        def window(c_ref, x_ref, w, hh):
            cols = slice(hh * HEAD_DIM, (hh + 1) * HEAD_DIM)
            rc = pl.multiple_of(jnp.minimum(w, n_cwin - 1) * DN_WIN, DN_WIN)
            rx = pl.multiple_of(jnp.maximum(w - n_cwin, 0) * DN_WIN, DN_WIN)
            return jnp.where(w < n_cwin, c_ref[0, pl.ds(rc, DN_WIN), cols], x_ref[0, pl.ds(rx, DN_WIN), cols])

        st = []
        for hh, dr in streams:
            rqw = rq[hh, wins[dr]]
            b0 = 6 * dr
            kb = window(kc_ref, kx_ref, wins[dr], hh)
            qb = window(qc_ref, qx_ref, wins[dr], hh)
            st.append(dict(
                hh=hh, dr=dr, rqw=rqw, b0=b0, kb=kb, qb=qb,
                kw=kb.astype(F32), qw=qb.astype(F32),
                vw=window(vc_ref, vx_ref, wins[dr], hh).astype(F32),
                beta=_lane_rep(rqw[b0:b0 + 1]),
                gcol=_lane_rep(rqw[b0 + 1:b0 + 2])))
        for x in st:
            x["kk"] = _dot_nt(x["kb"], x["kb"])
            x["qk"] = _dot_nt(x["qb"], x["kb"])
        recur_step(0)
        for x in st:
            b0, rqw = x["b0"], x["rqw"]
            incl = masks[0] if x["dr"] == 0 else masks[2]
            strict = masks[1] if x["dr"] == 0 else masks[3]
            gdiff = jnp.concatenate([x["gcol"]] * 2, axis=1) - rqw[b0 + 1:b0 + 2]
            e = jnp.exp(jnp.minimum(gdiff, 0.0))
            x["am"] = (x["qk"] * e * incl).astype(BF16)
            x["lm"] = x["kk"] * e * jnp.concatenate([x["beta"]] * 2, axis=1) * strict
            x["t"] = masks[4] - jnp.where(blkx == 1, x["lm"], 0.0)
        for lvl in range(1, INVERSE_LEVELS):
            half = 1 << lvl
            packed = half % 8 == 0
            for x in st:
                n_lvl = jnp.where((blkx >> lvl) == 1, x["lm"], 0.0)
                tb = x["t"].astype(BF16)
                if packed:
                    first = half if x["dr"] == 0 else 0
                    x["first"] = first
                    y = _dot(_take_rows(n_lvl, first, half).astype(BF16), tb)
                    x["y"] = _spread_rows(y, first, half).astype(BF16)
                else:
                    x["y"] = _dot(n_lvl.astype(BF16), tb).astype(BF16)
            for x in st:
                if packed:
                    first = x["first"]
                    zp = _dot(_take_rows(x["t"], first, half).astype(BF16), x["y"])
                    x["t"] = x["t"] - _spread_rows(zp, first, half)
                else:
                    x["t"] = x["t"] - _dot(x["t"].astype(BF16), x["y"])
            if lvl % 2 == 1 and 1 + lvl // 2 < CHUNKS_PER_WIN:
                recur_step(1 + lvl // 2)
        for x in st:
            b0, rqw = x["b0"], x["rqw"]
            bgam = _lane_rep(rqw[b0 + 3:b0 + 4])
            rhs = jnp.concatenate([x["vw"] * x["beta"], x["kw"] * bgam], axis=1).astype(BF16)
            x["uwb"] = _dot(x["t"].astype(BF16), rhs).astype(BF16)
        for x in st:
            hh, dr, b0, rqw = x["hh"], x["dr"], x["b0"], x["rqw"]
            auw = _dot(x["am"], x["uwb"])
            o0_s[slot, hh, dr] = auw[:, :HEAD_DIM]
            qe_s[slot, hh, dr] = (x["qw"] * _lane_rep(rqw[b0 + 2:b0 + 3]) - auw[:, HEAD_DIM:]).astype(BF16)
            dd_s[slot, hh, dr] = _lane_rep(rqw[b0 + 5:b0 + 6])
            x["kdt"] = jnp.transpose(x["kw"] * _lane_rep(rqw[b0 + 4:b0 + 5]))
        for c in range(CHUNKS_PER_WIN):
            for x in st:
                kdt_c = jnp.where(lane_chunk == c, x["kdt"], 0.0).astype(BF16)
                pr = _dot(kdt_c, x["uwb"])
                rr_s[slot, x["hh"], x["dr"], c] = pr[:, :HEAD_DIM]
                pp_s[slot, x["hh"], x["dr"], c] = (-pr[:, HEAD_DIM:]).astype(BF16)
        return tuple(state)

    s0 = jnp.zeros((HEAD_DIM, HEAD_DIM), F32)
    lax.fori_loop(0, n_win + 1, step, (s0,) * len(streams))

    for hh in range(hp):
        o_ref[0, :, hh * HEAD_DIM:(hh + 1) * HEAD_DIM] = oacc[hh, lc:z, :].astype(BF16)


def _deltanet(qkv_x, qkv_c, rows, a_log, dt_bias):
    nb, l, _ = qkv_x.shape
    lc = qkv_c.shape[1]
    z = lc + l
    n_win = z // DN_WIN
    hp = DN_HEADS_PER_STEP
    ng = DN_HEADS // hp
    gw = hp * HEAD_DIM

    def col(seq, off):
        return pl.BlockSpec((1, seq, gw), lambda b, h: (b, 0, off + h), pipeline_mode=pl.Buffered(1))

    smem = pl.BlockSpec(memory_space=pltpu.SMEM)
    slot_shape = (2, hp, 2)
    return pl.pallas_call(
        functools.partial(_dn_kernel, lc=lc, l=l, hp=hp),
        grid=(nb, ng),
        in_specs=[smem, smem,
                  col(l, 0), col(l, ng), col(l, 2 * ng), col(lc, 0), col(lc, ng), col(lc, 2 * ng),
                  pl.BlockSpec((1, hp, 8, z), lambda b, h: (b, h, 0, 0))],
        out_specs=pl.BlockSpec((1, l, gw), lambda b, h: (b, 0, h), pipeline_mode=pl.Buffered(1)),
        out_shape=jax.ShapeDtypeStruct((nb, l, DN_HEADS * HEAD_DIM), BF16),
        scratch_shapes=[pltpu.VMEM((hp, z, HEAD_DIM), F32),
                        pltpu.VMEM((hp, n_win, 16, DN_WIN), F32),
                        pltpu.VMEM((5, DN_WIN, DN_WIN), F32),
                        pltpu.VMEM(slot_shape + (DN_WIN, HEAD_DIM), BF16),
                        pltpu.VMEM(slot_shape + (DN_WIN, HEAD_DIM), F32),
                        pltpu.VMEM(slot_shape + (DN_WIN, HEAD_DIM), F32),
                        pltpu.VMEM(slot_shape + (CHUNKS_PER_WIN, HEAD_DIM, HEAD_DIM), BF16),
                        pltpu.VMEM(slot_shape + (CHUNKS_PER_WIN, HEAD_DIM, HEAD_DIM), F32)],
        compiler_params=_params("parallel", "parallel"),
        name="deltanet",
    )(a_log, dt_bias, qkv_x, qkv_x, qkv_x, qkv_c, qkv_c, qkv_c, rows)


def _attn_kernel(q_ref, k_ref, vt_ref, o_ref, *, tq, tk, n_kblk):
    q = q_ref[0]
    qm = jnp.concatenate([q[:, g * HEAD_DIM:(g + 1) * HEAD_DIM] for g in range(ATTN_GROUP)], axis=0)
    ncol = ATTN_GROUP * tq

    def scores(j):
        return _dot_nt(k_ref[0, j * tk:(j + 1) * tk, :], qm)

    def add_values(acc, pend):
        j, alpha, pb = pend
        return alpha * acc + _dot(vt_ref[0, 0, :, j * tk:(j + 1) * tk], pb)

    m = jnp.full((1, ncol), -jnp.inf, F32)
    acc = jnp.zeros((HEAD_DIM + ATTN_ONES_ROWS, ncol), F32)
    ahead = [scores(j) for j in range(min(ATTN_SCORE_LOOKAHEAD, n_kblk))]
    pend = None
    for j in range(n_kblk):
        s = ahead.pop(0)
        if j + ATTN_SCORE_LOOKAHEAD < n_kblk:
            ahead.append(scores(j + ATTN_SCORE_LOOKAHEAD))
        if pend is not None:
            acc = add_values(acc, pend)
        m_new = jnp.maximum(m, jnp.max(s, axis=0, keepdims=True))
        pend = (j, jnp.exp2(m - m_new), jnp.exp2(s - m_new).astype(BF16))
        m = m_new
    acc = add_values(acc, pend)
    ot = acc[:HEAD_DIM] / acc[HEAD_DIM:HEAD_DIM + 1]
    for g in range(ATTN_GROUP):
        o_ref[0, :, g * HEAD_DIM:(g + 1) * HEAD_DIM] = jnp.transpose(ot[:, g * tq:(g + 1) * tq]).astype(BF16)


def _attention(q, k_all, vt_all, tq, tk):
    nb, l, _ = q.shape
    z = k_all.shape[1]
    gw = ATTN_GROUP * HEAD_DIM
    return pl.pallas_call(
        functools.partial(_attn_kernel, tq=tq, tk=tk, n_kblk=z // tk),
        grid=(nb, ATTN_KV_HEADS, l // tq),
        in_specs=[pl.BlockSpec((1, tq, gw), lambda b, kv, i: (b, i, kv)),
                  pl.BlockSpec((1, z, HEAD_DIM), lambda b, kv, i: (b, 0, kv)),
                  pl.BlockSpec((1, 1, HEAD_DIM + ATTN_ONES_ROWS, z), lambda b, kv, i: (b, kv, 0, 0))],
        out_specs=pl.BlockSpec((1, tq, gw), lambda b, kv, i: (b, i, kv)),
        out_shape=jax.ShapeDtypeStruct((nb, l, ATTN_HEADS * HEAD_DIM), BF16),
        compiler_params=_params("parallel", "parallel", "parallel"),
        name="gqa_attn",
    )(q, k_all, vt_all)


def _merge_kernel(x_ref, gate_ref, odn_ref, z_ref, oat_ref, g_ref, dnn_ref, post_ref,
                  wdn_ref, wat_ref, wo_ref, o_ref):
    x = x_ref[0]
    d = x.shape[-1]
    y_at = _dot(oat_ref[0], wat_ref[...])
    odn = odn_ref[0].astype(F32)
    zz = z_ref[0].astype(F32)
    parts = []
    for hd in range(DN_HEADS):
        sl = slice(hd * HEAD_DIM, (hd + 1) * HEAD_DIM)
        parts.append((_rms(odn[:, sl], dnn_ref[...]) * jax.nn.silu(zz[:, sl])).astype(BF16))
    y_dn = _dot(jnp.concatenate(parts, axis=1), wdn_ref[...])
    g = g_ref[0].astype(F32)
    y = jax.nn.sigmoid(g[:, :d]) * y_dn + jax.nn.sigmoid(g[:, d:]) * y_at
    yy = _dot(y.astype(BF16), wo_ref[...])
    o_ref[0] = x + gate_ref[0] * _rms(yy, post_ref[...])


def _merge(x, gate, o_dn, z, o_at, gates, dn_norm, post_g, w_dn, w_at, w_o, tm):
    nb, l, d = x.shape

    def tok(n):
        return pl.BlockSpec((1, tm, n), lambda b, i: (b, i, 0))

    def full(a):
        return pl.BlockSpec(a.shape, lambda b, i: (0,) * a.ndim)

    return pl.pallas_call(
        _merge_kernel,
        grid=(nb, l // tm),
        in_specs=[tok(d), pl.BlockSpec((1, 1, d), lambda b, i: (b, 0, 0)), tok(o_dn.shape[-1]), tok(z.shape[-1]),
                  tok(o_at.shape[-1]), tok(gates.shape[-1]), full(dn_norm), full(post_g),
                  full(w_dn), full(w_at), full(w_o)],
        out_specs=tok(d),
        out_shape=jax.ShapeDtypeStruct((nb, l, d), F32),
        compiler_params=_params("parallel", "parallel"),
        name="mixer_merge",
    )(x, gate, o_dn, z, o_at, gates, dn_norm, post_g, w_dn, w_at, w_o)


def _rope_tables(l):
    rows = l // GRID_W
    row = jnp.repeat(jnp.arange(rows, dtype=jnp.int32), GRID_W)
    colp = jnp.tile(jnp.arange(GRID_W, dtype=jnp.int32), rows)
    pos = jnp.stack([row, colp], axis=-1).astype(F32)
    n_freq = HEAD_DIM // 4
    inv_freq = ROPE_THETA ** (-jnp.arange(n_freq, dtype=F32) / n_freq)
    ang = pos[:, :, None] * inv_freq
    cos, sin = jnp.cos(ang), jnp.sin(ang)
    cosf = jnp.concatenate([cos[:, 0], cos[:, 0], cos[:, 1], cos[:, 1]], axis=-1)
    sinf = jnp.concatenate([-sin[:, 0], sin[:, 0], -sin[:, 1], sin[:, 1]], axis=-1)
    return cosf, sinf


def _split_w_in(w_in):
    d = w_in.shape[0]
    dn_qkv = 3 * DN_HEADS * HEAD_DIM
    dn_v = DN_HEADS * HEAD_DIM
    n_ba = 4 * DN_HEADS
    a_q = ATTN_HEADS * HEAD_DIM
    a_kv = ATTN_KV_HEADS * HEAD_DIM
    sizes = (dn_qkv, dn_v, n_ba, a_q, a_kv, a_kv, 2 * d)
    offs = [0]
    for sz in sizes:
        offs.append(offs[-1] + sz)
    assert offs[-1] == w_in.shape[1]
    wb = w_in.astype(BF16)
    part = [wb[:, offs[i]:offs[i + 1]] for i in range(len(sizes))]
    return {"qkv": part[0], "z": part[1], "bat": part[2].T, "q": part[3], "k": part[4],
            "vt": part[5].T, "g": part[6]}


def kernel(x, c, ctx, c_ctx, w_mod, b_mod, ffn1_pre, ffn1_post, ffn1_w_gu, ffn1_w_down, mix_pre, mix_post, w_in, dn_conv, dn_a_log, dn_dt_bias, dn_out_norm, w_dn_out, attn_q_norm, attn_k_norm, w_attn_out, w_out, ffn2_pre, ffn2_post, ffn2_w_gu, ffn2_w_down):
    nb, l, d = x.shape
    lc = ctx.shape[1]
    assert w_mod.shape[0] == 1, "single-layer block: the context stream is never updated"
    assert l % DN_WIN == 0 and lc % DN_WIN == 0 and l % GRID_W == 0
    tm_x = min(512, l)
    tm_c = min(256, lc)

    pad = (-(nb + 1)) % 8
    cc = jnp.concatenate([c, c_ctx[None, :], jnp.zeros((pad, d), F32)], axis=0)
    mod_all = _modulation(cc, w_mod[0], b_mod[0])
    mod = [mod_all[:nb, j * d:(j + 1) * d][:, None, :] for j in range(N_MOD)]
    mod_c = [jnp.broadcast_to(mod_all[nb:nb + 1, j * d:(j + 1) * d][:, None, :], (nb, 1, d)) for j in range(N_MOD)]

    f1 = _ffn_weights(ffn1_w_gu[0], ffn1_w_down[0])
    x1 = _ffn(x, mod[0], mod[1], mod[2], ffn1_pre, ffn1_post, *f1, tm=tm_x)
    c1 = _ffn(ctx, mod_c[0], mod_c[1], mod_c[2], ffn1_pre, ffn1_post, *f1, tm=tm_c)

    w = _split_w_in(w_in[0])
    rope = _rope_tables(l)
    qkv_x, z_x, qa_x, ka_x, vt_x, g_x, bat_x = _inproj(
        x1, mod[3], mod[4], mix_pre, w, attn_q_norm, attn_k_norm, dn_conv[0], rope, tm=min(256, l))
    qkv_c, _, _, ka_c, vt_c, _, bat_c = _inproj(
        c1, mod_c[3], mod_c[4], mix_pre, w, attn_q_norm, attn_k_norm, dn_conv[0], None, tm=tm_c)

    z = lc + l
    bat = jnp.concatenate([bat_c, bat_x], axis=-1)
    rows = bat.reshape(nb, 2, 2, DN_HEADS, z).transpose(0, 3, 1, 2, 4).reshape(nb, DN_HEADS, 4, z)
    rows = jnp.concatenate([rows, jnp.zeros_like(rows)], axis=2)
    o_dn = _deltanet(qkv_x, qkv_c, rows, dn_a_log[0], dn_dt_bias[0])

    k_all = jnp.concatenate([ka_c, ka_x], axis=1)
    vt_all = jnp.concatenate([vt_c, vt_x], axis=2).reshape(nb, ATTN_KV_HEADS, HEAD_DIM, z)
    vt_all = jnp.concatenate([vt_all, jnp.ones((nb, ATTN_KV_HEADS, ATTN_ONES_ROWS, z), BF16)], axis=2)
    o_at = _attention(qa_x, k_all, vt_all, tq=min(256, l), tk=DN_WIN)

    x2 = _merge(x1, mod[5], o_dn, z_x, o_at, g_x, dn_out_norm, mix_post,
                w_dn_out[0].astype(BF16), w_attn_out[0].astype(BF16), w_out[0].astype(BF16), tm=tm_x)

    f2 = _ffn_weights(ffn2_w_gu[0], ffn2_w_down[0])
    return _ffn(x2, mod[6], mod[7], mod[8], ffn2_pre, ffn2_post, *f2, tm=tm_x)
```

```python
import functools

import jax
import jax.numpy as jnp
from jax import lax
from jax.experimental import pallas as pl
from jax.experimental.pallas import tpu as pltpu

F32 = jnp.float32
BF16 = jnp.bfloat16

EPS = 1e-6
GRID_W = 64
ROPE_THETA = 10000.0
N_MOD = 9
FFN_RES_WEIGHT = 0.5

HEAD_DIM = 128
DN_HEADS = 8
ATTN_HEADS = 8
ATTN_KV_HEADS = 2
ATTN_GROUP = ATTN_HEADS // ATTN_KV_HEADS
DN_CHUNK = 128
DN_WIN = 256
CHUNKS_PER_WIN = DN_WIN // DN_CHUNK
DN_HEADS_PER_STEP = 4
INVERSE_LEVELS = 7
FFN_HIDDEN_CHUNK = 256
CONV_HALO = 8
ATTN_SCORE_LOOKAHEAD = 2
ATTN_ONES_ROWS = 16
ATTN_Q_SCALE = HEAD_DIM ** -0.5 * 1.4426950408889634

VMEM_LIMIT_BYTES = 56 * 1024 * 1024


def _dot(a, b):
    return jnp.dot(a, b, preferred_element_type=F32)


def _dot_nt(a, b):
    return lax.dot_general(a, b, (((1,), (1,)), ((), ())), preferred_element_type=F32)


def _rms(x, g):
    return x * lax.rsqrt(jnp.mean(x * x, axis=-1, keepdims=True) + EPS) * g


def _params(*sem):
    return pltpu.CompilerParams(dimension_semantics=sem, vmem_limit_bytes=VMEM_LIMIT_BYTES)


def _mod_kernel(c_ref, w_ref, b_ref, o_ref):
    a = jax.nn.silu(c_ref[...])
    o_ref[...] = jnp.dot(a, w_ref[...], preferred_element_type=F32,
                         precision=lax.Precision.HIGHEST) + b_ref[...]


def _modulation(cc, w_mod, b_mod):
    rows, d = cc.shape
    n = w_mod.shape[1]
    tn = d
    return pl.pallas_call(
        _mod_kernel,
        grid=(n // tn,),
        in_specs=[pl.BlockSpec((rows, d), lambda j: (0, 0)),
                  pl.BlockSpec((d, tn), lambda j: (0, j)),
                  pl.BlockSpec((1, tn), lambda j: (0, j))],
        out_specs=pl.BlockSpec((rows, tn), lambda j: (0, j)),
        out_shape=jax.ShapeDtypeStruct((rows, n), F32),
        compiler_params=_params("arbitrary"),
        name="adaln_mod",
    )(cc, w_mod, b_mod.reshape(1, n))


def _ffn_kernel(x_ref, sh_ref, sc_ref, gt_ref, pre_ref, post_ref, wg_ref, wu_ref, wd_ref, o_ref, *, n_chunks):
    x = x_ref[0]
    h = _rms(x, pre_ref[...]) * (1.0 + sc_ref[0]) + sh_ref[0]
    hb = h.astype(BF16)
    acc = None
    hc = FFN_HIDDEN_CHUNK
    for j in range(n_chunks):
        a = _dot(hb, wg_ref[:, j * hc:(j + 1) * hc])
        b = _dot(hb, wu_ref[:, j * hc:(j + 1) * hc])
        act = (jax.nn.silu(a) * b).astype(BF16)
        d = _dot(act, wd_ref[j * hc:(j + 1) * hc, :])
        acc = d if acc is None else acc + d
    o_ref[0] = x + FFN_RES_WEIGHT * gt_ref[0] * _rms(acc, post_ref[...])


def _ffn(x, shift, scale, gate, pre_g, post_g, wg, wu, wd, tm):
    nb, l, d = x.shape
    hid = wg.shape[1]
    n_chunks = hid // FFN_HIDDEN_CHUNK
    row = pl.BlockSpec((1, tm, d), lambda b, i: (b, i, 0))
    per_b = pl.BlockSpec((1, 1, d), lambda b, i: (b, 0, 0))
    vec = pl.BlockSpec((1, d), lambda b, i: (0, 0))
    w_up = pl.BlockSpec((d, hid), lambda b, i: (0, 0))
    w_dn = pl.BlockSpec((hid, d), lambda b, i: (0, 0))
    return pl.pallas_call(
        functools.partial(_ffn_kernel, n_chunks=n_chunks),
        grid=(nb, l // tm),
        in_specs=[row, per_b, per_b, per_b, vec, vec, w_up, w_up, w_dn],
        out_specs=row,
        out_shape=jax.ShapeDtypeStruct((nb, l, d), F32),
        compiler_params=_params("parallel", "parallel"),
        name="ffn",
    )(x, shift, scale, gate, pre_g, post_g, wg, wu, wd)


def _ffn_weights(w_gu, w_down):
    hid = w_gu.shape[1] // 2
    assert hid % FFN_HIDDEN_CHUNK == 0
    return w_gu[:, :hid].astype(BF16), w_gu[:, hid:].astype(BF16), w_down.astype(BF16)


def _head_norm_rope(r, gain, n_heads, scale, rope):
    out = []
    if rope is not None:
        cosf, sinf = rope
        lane = lax.broadcasted_iota(jnp.int32, (1, HEAD_DIM), 1)
        first_half = (lane % (HEAD_DIM // 2)) < (HEAD_DIM // 4)
    for hd in range(n_heads):
        xh = r[:, hd * HEAD_DIM:(hd + 1) * HEAD_DIM]
        xh = _rms(xh, gain)
        if rope is not None:
            up = pltpu.roll(xh, HEAD_DIM - HEAD_DIM // 4, axis=1)
            dn = pltpu.roll(xh, HEAD_DIM // 4, axis=1)
            xh = xh * cosf + jnp.where(first_half, up, dn) * sinf
        if scale != 1.0:
            xh = xh * scale
        out.append(xh)
    return out


def _inproj_kernel(*refs, use_rope, seq_len):
    if use_rope:
        (x_ref, xp_ref, xn_ref, sh_ref, sc_ref, pre_ref, wqkv_ref, wz_ref, wq_ref, wk_ref, wvt_ref, wg_ref,
         wbat_ref, qn_ref, kn_ref, cw_ref, cos_ref, sin_ref, oqkv, oz, oq, ok, ovt, og, obat) = refs
        rope = (cos_ref[...], sin_ref[...])
    else:
        (x_ref, xp_ref, xn_ref, sh_ref, sc_ref, pre_ref, wqkv_ref, wz_ref, wq_ref, wk_ref, wvt_ref, wg_ref,
         wbat_ref, qn_ref, kn_ref, cw_ref, oqkv, oz, oq, ok, ovt, og, obat) = refs
        rope = None
    tm, d = x_ref.shape[1], x_ref.shape[2]
    xe = jnp.concatenate([xp_ref[0], x_ref[0], xn_ref[0]], axis=0)
    he = _rms(xe, pre_ref[...]) * (1.0 + sc_ref[0]) + sh_ref[0]
    hbe = he.astype(BF16)
    hb = he[CONV_HALO:CONV_HALO + tm].astype(BF16)

    tpos = pl.program_id(1) * tm + lax.broadcasted_iota(jnp.int32, (tm, 1), 0)

    def dn_epilogue(j):
        def run(pre):
            cw = cw_ref[:, j * d:(j + 1) * d]
            prev = jnp.where(tpos == 0, 0.0, pre[CONV_HALO - 1:CONV_HALO - 1 + tm])
            nxt = jnp.where(tpos == seq_len - 1, 0.0, pre[CONV_HALO + 1:CONV_HALO + 1 + tm])
            y = jax.nn.silu(prev * cw[0:1] + pre[CONV_HALO:CONV_HALO + tm] * cw[1:2] + nxt * cw[2:3])
            for hd in range(d // HEAD_DIM):
                yh = y[:, hd * HEAD_DIM:(hd + 1) * HEAD_DIM]
                if j < 2:
                    yh = yh * lax.rsqrt(jnp.sum(yh * yh, axis=-1, keepdims=True) + EPS)
                    if j == 0:
                        yh = yh * HEAD_DIM ** -0.5
                oqkv[0, :, j * d + hd * HEAD_DIM:j * d + (hd + 1) * HEAD_DIM] = yh.astype(BF16)
        return run

    def store_cast(o_ref, j):
        def run(r):
            o_ref[0, :, j * d:(j + 1) * d] = r.astype(BF16)
        return run

    def attn_epilogue(o_ref, gain_ref, n_heads, scale):
        def run(r):
            hs = _head_norm_rope(r, gain_ref[...], n_heads, scale, rope)
            for hd in range(n_heads):
                o_ref[0, :, hd * HEAD_DIM:(hd + 1) * HEAD_DIM] = hs[hd].astype(BF16)
        return run

    def store_vt(r):
        ovt[0] = r.astype(BF16)

    def store_bat(r):
        obat[0] = r

    stages = [(functools.partial(_dot, hbe, wqkv_ref[:, j * d:(j + 1) * d]), dn_epilogue(j))
              for j in range(wqkv_ref.shape[1] // d)]
    stages.append((lambda: _dot(hb, wq_ref[...]), attn_epilogue(oq, qn_ref, ATTN_HEADS, ATTN_Q_SCALE)))
    stages.append((lambda: _dot(hb, wk_ref[...]), attn_epilogue(ok, kn_ref, ATTN_KV_HEADS, 1.0)))
    stages.append((lambda: _dot(hb, wz_ref[...]), store_cast(oz, 0)))
    stages += [(functools.partial(_dot, hb, wg_ref[:, j * d:(j + 1) * d]), store_cast(og, j))
               for j in range(wg_ref.shape[1] // d)]
    stages.append((lambda: _dot_nt(wvt_ref[...], hb), store_vt))
    stages.append((lambda: _dot_nt(wbat_ref[...], hb), store_bat))
    result = stages[0][0]()
    for j, (_, epilogue) in enumerate(stages):
        upcoming = stages[j + 1][0]() if j + 1 < len(stages) else None
        epilogue(result)
        result = upcoming


def _inproj(x, shift, scale, pre_g, w, q_norm, k_norm, conv_w, rope, tm):
    nb, l, d = x.shape
    use_rope = rope is not None
    row = pl.BlockSpec((1, tm, d), lambda b, i: (b, i, 0))
    hpt = tm // CONV_HALO
    n_halo = l // CONV_HALO
    row_prev = pl.BlockSpec((1, CONV_HALO, d), lambda b, i: (b, jnp.maximum(i * hpt - 1, 0), 0))
    row_next = pl.BlockSpec((1, CONV_HALO, d), lambda b, i: (b, jnp.minimum((i + 1) * hpt, n_halo - 1), 0))
    per_b = pl.BlockSpec((1, 1, d), lambda b, i: (b, 0, 0))

    def full(a):
        return pl.BlockSpec(a.shape, lambda b, i: (0,) * a.ndim)

    ins = [x, x, x, shift, scale, pre_g, w["qkv"], w["z"], w["q"], w["k"], w["vt"], w["g"], w["bat"],
           q_norm, k_norm, conv_w]
    in_specs = [row, row_prev, row_next, per_b, per_b] + [full(a) for a in ins[5:]]
    if use_rope:
        ins += list(rope)
        in_specs += [pl.BlockSpec((tm, HEAD_DIM), lambda b, i: (i, 0))] * 2
    n_qkv, n_z, n_q, n_k = w["qkv"].shape[1], w["z"].shape[1], w["q"].shape[1], w["k"].shape[1]
    n_vt, n_g, n_ba = w["vt"].shape[0], w["g"].shape[1], w["bat"].shape[0]

    def tok(n):
        return pl.BlockSpec((1, tm, n), lambda b, i: (b, i, 0))

    def chan(n):
        return pl.BlockSpec((1, n, tm), lambda b, i: (b, 0, i))

    out_shape = [jax.ShapeDtypeStruct((nb, l, n_qkv), BF16), jax.ShapeDtypeStruct((nb, l, n_z), BF16),
                 jax.ShapeDtypeStruct((nb, l, n_q), BF16), jax.ShapeDtypeStruct((nb, l, n_k), BF16),
                 jax.ShapeDtypeStruct((nb, n_vt, l), BF16), jax.ShapeDtypeStruct((nb, l, n_g), BF16),
                 jax.ShapeDtypeStruct((nb, n_ba, l), F32)]
    out_specs = [tok(n_qkv), tok(n_z), tok(n_q), tok(n_k), chan(n_vt), tok(n_g), chan(n_ba)]
    return pl.pallas_call(
        functools.partial(_inproj_kernel, use_rope=use_rope, seq_len=l),
        grid=(nb, l // tm),
        in_specs=in_specs,
        out_specs=out_specs,
        out_shape=out_shape,
        compiler_params=_params("parallel", "parallel"),
        name="mixer_inproj",
    )(*ins)


def _softplus(x):
    return jnp.maximum(x, 0.0) + jnp.log1p(jnp.exp(-jnp.abs(x)))


def _split3(x):
    x1 = x.astype(BF16)
    r1 = x - x1.astype(F32)
    x2 = r1.astype(BF16)
    x3 = (r1 - x2.astype(F32)).astype(BF16)
    return x1, x2, x3


def _lane_rep(row):
    return jnp.transpose(jnp.broadcast_to(row, (HEAD_DIM, DN_WIN)))


def _take_rows(a, first, half):
    return jnp.concatenate([a[r:r + half] for r in range(first, a.shape[0], 2 * half)], axis=0)


def _spread_rows(p, first, half):
    zero = jnp.zeros((half, p.shape[1]), p.dtype)
    out = []
    for i in range(p.shape[0] // half):
        blk = p[i * half:(i + 1) * half]
        out += [zero, blk] if first else [blk, zero]
    return jnp.concatenate(out, axis=0)


def _dn_kernel(alog_ref, dtb_ref, qx_ref, kx_ref, vx_ref, qc_ref, kc_ref, vc_ref, rows_ref, o_ref,
               oacc, rq, masks, qe_s, o0_s, dd_s, pp_s, rr_s, *, lc, l, hp):
    z = lc + l
    n_win = z // DN_WIN
    n_cwin = lc // DN_WIN
    hd0 = pl.program_id(1) * hp

    ri = lax.broadcasted_iota(jnp.int32, (DN_WIN, DN_WIN), 0)
    ci = lax.broadcasted_iota(jnp.int32, (DN_WIN, DN_WIN), 1)
    same = (ri // DN_CHUNK) == (ci // DN_CHUNK)
    masks[0] = (same & (ri >= ci)).astype(F32)
    masks[1] = (same & (ri > ci)).astype(F32)
    masks[2] = (same & (ri <= ci)).astype(F32)
    masks[3] = (same & (ri < ci)).astype(F32)
    masks[4] = (ri == ci).astype(F32)

    for hh in range(hp):
        neg_a = [-jnp.exp(jnp.full((1, DN_WIN), alog_ref[dr, hd0 + hh], F32)) for dr in range(2)]
        dtb = [jnp.full((1, DN_WIN), dtb_ref[dr, hd0 + hh], F32) for dr in range(2)]
        for w in range(n_win):
            r = rows_ref[0, hh, :, w * DN_WIN:(w + 1) * DN_WIN]
            beta = [jax.nn.sigmoid(r[dr:dr + 1]) for dr in range(2)]
            g = [neg_a[dr] * _softplus(r[2 + dr:3 + dr] + dtb[dr]) for dr in range(2)]
            parts = _split3(g[0]) + _split3(g[1])
            lhs = jnp.concatenate(list(parts) + [jnp.zeros((10, DN_WIN), BF16)], axis=0)
            pre = _dot(lhs, masks[2].astype(BF16))
            suf = _dot(lhs, masks[0].astype(BF16))
            pre = [pre[0:1] + pre[1:2] + pre[2:3], pre[3:4] + pre[4:5] + pre[5:6]]
            suf = [suf[0:1] + suf[1:2] + suf[2:3], suf[3:4] + suf[4:5] + suf[5:6]]
            gcum = [pre[0], suf[1]]
            rest = [suf[0] - g[0], pre[1] - g[1]]
            rows = []
            for dr in range(2):
                gam = jnp.exp(gcum[dr])
                rows += [beta[dr], gcum[dr], gam, beta[dr] * gam, jnp.exp(rest[dr]),
                         jnp.exp(gcum[dr] + rest[dr])]
            rows.append(jnp.zeros((4, DN_WIN), F32))
            rq[hh, w] = jnp.concatenate(rows, axis=0)

    oacc[...] = jnp.zeros(oacc.shape, F32)

    lane_chunk = lax.broadcasted_iota(jnp.int32, (1, DN_WIN), 1) // DN_CHUNK
    streams = [(hh, dr) for hh in range(hp) for dr in range(2)]

    def bwd_window(i):
        return jnp.where(i < n_cwin, n_cwin - 1 - i, n_win - 1 - (i - n_cwin))

    zero_slot = 1
    qe_s[zero_slot] = jnp.zeros(qe_s.shape[1:], BF16)
    o0_s[zero_slot] = jnp.zeros(o0_s.shape[1:], F32)
    dd_s[zero_slot] = jnp.zeros(dd_s.shape[1:], F32)
    pp_s[zero_slot] = jnp.zeros(pp_s.shape[1:], BF16)
    rr_s[zero_slot] = jnp.zeros(rr_s.shape[1:], F32)

    def step(i, carry):
        state = list(carry)
        cur = jnp.minimum(i, n_win - 1)
        prev = jnp.maximum(i - 1, 0)
        slot, pslot = i % 2, (i + 1) % 2
        wins = (cur, bwd_window(cur))
        pr0 = [pl.multiple_of(w * DN_WIN, DN_WIN) for w in (prev, bwd_window(prev))]

        def recur_step(k):
            cs = (k, CHUNKS_PER_WIN - 1 - k)
            blk = DN_CHUNK + HEAD_DIM
            for hh in range(hp):
                lhs = []
                for dr in range(2):
                    lo = cs[dr] * DN_CHUNK
                    lhs += [qe_s[pslot, hh, dr, lo:lo + DN_CHUNK, :], pp_s[pslot, hh, dr, cs[dr]]]
                s_pair = state[2 * hh:2 * hh + 2]
                res = _dot(jnp.concatenate(lhs, axis=0), jnp.concatenate(s_pair, axis=1).astype(BF16))
                for dr in range(2):
                    lo = cs[dr] * DN_CHUNK
                    sub = res[dr * blk:(dr + 1) * blk, dr * HEAD_DIM:(dr + 1) * HEAD_DIM]
                    rows = pl.ds(pr0[dr] + lo, DN_CHUNK)
                    oacc[hh, rows, :] = (oacc[hh, rows, :] + o0_s[pslot, hh, dr, lo:lo + DN_CHUNK, :]
                                         + sub[:DN_CHUNK])
                    dv = dd_s[pslot, hh, dr, lo:lo + 8, :]
                    state[2 * hh + dr] = (jnp.concatenate([dv] * (HEAD_DIM // 8), axis=0) * s_pair[dr]
                                          + sub[DN_CHUNK:] + rr_s[pslot, hh, dr, cs[dr]])

        blkx = (lax.broadcasted_iota(jnp.int32, (DN_WIN, DN_WIN), 0)
                ^ lax.broadcasted_iota(jnp.int32, (DN_WIN, DN_WIN), 1))

        def window(c_ref, x_ref, w, hh):
            cols = slice(hh * HEAD_DIM, (hh + 1) * HEAD_DIM)
            rc = pl.multiple_of(jnp.minimum(w, n_cwin - 1) * DN_WIN, DN_WIN)
            rx = pl.multiple_of(jnp.maximum(w - n_cwin, 0) * DN_WIN, DN_WIN)
            return jnp.where(w < n_cwin, c_ref[0, pl.ds(rc, DN_WIN), cols], x_ref[0, pl.ds(rx, DN_WIN), cols])

        st = []
        for hh, dr in streams:
            rqw = rq[hh, wins[dr]]
            b0 = 6 * dr
            kb = window(kc_ref, kx_ref, wins[dr], hh)
            qb = window(qc_ref, qx_ref, wins[dr], hh)
            st.append(dict(
                hh=hh, dr=dr, rqw=rqw, b0=b0, kb=kb, qb=qb,
                kw=kb.astype(F32), qw=qb.astype(F32),
                vw=window(vc_ref, vx_ref, wins[dr], hh).astype(F32),
                beta=_lane_rep(rqw[b0:b0 + 1]),
                gcol=_lane_rep(rqw[b0 + 1:b0 + 2])))
        for x in st:
            x["kk"] = _dot_nt(x["kb"], x["kb"])
            x["qk"] = _dot_nt(x["qb"], x["kb"])
        recur_step(0)
        for x in st:
            b0, rqw = x["b0"], x["rqw"]
            incl = masks[0] if x["dr"] == 0 else masks[2]
            strict = masks[1] if x["dr"] == 0 else masks[3]
            gdiff = jnp.concatenate([x["gcol"]] * 2, axis=1) - rqw[b0 + 1:b0 + 2]
            e = jnp.exp(jnp.minimum(gdiff, 0.0))
            x["am"] = (x["qk"] * e * incl).astype(BF16)
            x["lm"] = x["kk"] * e * jnp.concatenate([x["beta"]] * 2, axis=1) * strict
            x["t"] = masks[4] - jnp.where(blkx == 1, x["lm"], 0.0)
        for lvl in range(1, INVERSE_LEVELS):
            half = 1 << lvl
            packed = half % 8 == 0
            for x in st:
                n_lvl = jnp.where((blkx >> lvl) == 1, x["lm"], 0.0)
                tb = x["t"].astype(BF16)
                if packed:
                    first = half if x["dr"] == 0 else 0
                    x["first"] = first
                    y = _dot(_take_rows(n_lvl, first, half).astype(BF16), tb)
                    x["y"] = _spread_rows(y, first, half).astype(BF16)
                else:
                    x["y"] = _dot(n_lvl.astype(BF16), tb).astype(BF16)
            for x in st:
                if packed:
                    first = x["first"]
                    zp = _dot(_take_rows(x["t"], first, half).astype(BF16), x["y"])
                    x["t"] = x["t"] - _spread_rows(zp, first, half)
                else:
                    x["t"] = x["t"] - _dot(x["t"].astype(BF16), x["y"])
            if lvl % 2 == 1 and 1 + lvl // 2 < CHUNKS_PER_WIN:
                recur_step(1 + lvl // 2)
        for x in st:
            b0, rqw = x["b0"], x["rqw"]
            bgam = _lane_rep(rqw[b0 + 3:b0 + 4])
            rhs = jnp.concatenate([x["vw"] * x["beta"], x["kw"] * bgam], axis=1).astype(BF16)
            x["uwb"] = _dot(x["t"].astype(BF16), rhs).astype(BF16)
        for x in st:
            hh, dr, b0, rqw = x["hh"], x["dr"], x["b0"], x["rqw"]
            auw = _dot(x["am"], x["uwb"])
            o0_s[slot, hh, dr] = auw[:, :HEAD_DIM]
            qe_s[slot, hh, dr] = (x["qw"] * _lane_rep(rqw[b0 + 2:b0 + 3]) - auw[:, HEAD_DIM:]).astype(BF16)
            dd_s[slot, hh, dr] = _lane_rep(rqw[b0 + 5:b0 + 6])
            x["kdt"] = jnp.transpose(x["kw"] * _lane_rep(rqw[b0 + 4:b0 + 5]))
        for c in range(CHUNKS_PER_WIN):
            for x in st:
                kdt_c = jnp.where(lane_chunk == c, x["kdt"], 0.0).astype(BF16)
                pr = _dot(kdt_c, x["uwb"])
                rr_s[slot, x["hh"], x["dr"], c] = pr[:, :HEAD_DIM]
                pp_s[slot, x["hh"], x["dr"], c] = (-pr[:, HEAD_DIM:]).astype(BF16)
        return tuple(state)

    s0 = jnp.zeros((HEAD_DIM, HEAD_DIM), F32)
    lax.fori_loop(0, n_win + 1, step, (s0,) * len(streams))

    for hh in range(hp):
        o_ref[0, :, hh * HEAD_DIM:(hh + 1) * HEAD_DIM] = oacc[hh, lc:z, :].astype(BF16)


def _deltanet(qkv_x, qkv_c, rows, a_log, dt_bias):
    nb, l, _ = qkv_x.shape
    lc = qkv_c.shape[1]
    z = lc + l
    n_win = z // DN_WIN
    hp = DN_HEADS_PER_STEP
    ng = DN_HEADS // hp
    gw = hp * HEAD_DIM

    def col(seq, off):
        return pl.BlockSpec((1, seq, gw), lambda b, h: (b, 0, off + h), pipeline_mode=pl.Buffered(1))

    smem = pl.BlockSpec(memory_space=pltpu.SMEM)
    slot_shape = (2, hp, 2)
    return pl.pallas_call(
        functools.partial(_dn_kernel, lc=lc, l=l, hp=hp),
        grid=(nb, ng),
        in_specs=[smem, smem,
                  col(l, 0), col(l, ng), col(l, 2 * ng), col(lc, 0), col(lc, ng), col(lc, 2 * ng),
                  pl.BlockSpec((1, hp, 8, z), lambda b, h: (b, h, 0, 0))],
        out_specs=pl.BlockSpec((1, l, gw), lambda b, h: (b, 0, h), pipeline_mode=pl.Buffered(1)),
        out_shape=jax.ShapeDtypeStruct((nb, l, DN_HEADS * HEAD_DIM), BF16),
        scratch_shapes=[pltpu.VMEM((hp, z, HEAD_DIM), F32),
                        pltpu.VMEM((hp, n_win, 16, DN_WIN), F32),
                        pltpu.VMEM((5, DN_WIN, DN_WIN), F32),
                        pltpu.VMEM(slot_shape + (DN_WIN, HEAD_DIM), BF16),
                        pltpu.VMEM(slot_shape + (DN_WIN, HEAD_DIM), F32),
                        pltpu.VMEM(slot_shape + (DN_WIN, HEAD_DIM), F32),
                        pltpu.VMEM(slot_shape + (CHUNKS_PER_WIN, HEAD_DIM, HEAD_DIM), BF16),
                        pltpu.VMEM(slot_shape + (CHUNKS_PER_WIN, HEAD_DIM, HEAD_DIM), F32)],
        compiler_params=_params("parallel", "parallel"),
        name="deltanet",
    )(a_log, dt_bias, qkv_x, qkv_x, qkv_x, qkv_c, qkv_c, qkv_c, rows)


def _attn_kernel(q_ref, kc_ref, kx_ref, vtc_ref, vtx_ref, o_ref, *, tq, tk, n_cblk, n_kblk):
    q = q_ref[0]
    qm = jnp.concatenate([q[:, g * HEAD_DIM:(g + 1) * HEAD_DIM] for g in range(ATTN_GROUP)], axis=0)
    ncol = ATTN_GROUP * tq
    ones = jnp.ones((ATTN_ONES_ROWS, tk), BF16)

    def key_block(j):
        if j < n_cblk:
            return kc_ref[0, j * tk:(j + 1) * tk, :], vtc_ref[0, :, j * tk:(j + 1) * tk]
        j -= n_cblk
        return kx_ref[0, j * tk:(j + 1) * tk, :], vtx_ref[0, :, j * tk:(j + 1) * tk]

    def scores(j):
        return _dot_nt(key_block(j)[0], qm)

    def add_values(acc, pend):
        j, alpha, pb = pend
        return alpha * acc + _dot(jnp.concatenate([key_block(j)[1], ones], axis=0), pb)

    m = jnp.full((1, ncol), -jnp.inf, F32)
    acc = jnp.zeros((HEAD_DIM + ATTN_ONES_ROWS, ncol), F32)
    ahead = [scores(j) for j in range(min(ATTN_SCORE_LOOKAHEAD, n_kblk))]
    pend = None
    for j in range(n_kblk):
        s = ahead.pop(0)
        if j + ATTN_SCORE_LOOKAHEAD < n_kblk:
            ahead.append(scores(j + ATTN_SCORE_LOOKAHEAD))
        if pend is not None:
            acc = add_values(acc, pend)
        m_new = jnp.maximum(m, jnp.max(s, axis=0, keepdims=True))
        pend = (j, jnp.exp2(m - m_new), jnp.exp2(s - m_new).astype(BF16))
        m = m_new
    acc = add_values(acc, pend)
    ot = acc[:HEAD_DIM] / acc[HEAD_DIM:HEAD_DIM + 1]
    for g in range(ATTN_GROUP):
        o_ref[0, :, g * HEAD_DIM:(g + 1) * HEAD_DIM] = jnp.transpose(ot[:, g * tq:(g + 1) * tq]).astype(BF16)


def _attention(q, k_c, k_x, vt_c, vt_x, tq, tk):
    nb, l, _ = q.shape
    lc = k_c.shape[1]
    gw = ATTN_GROUP * HEAD_DIM

    def keys(seq):
        return pl.BlockSpec((1, seq, HEAD_DIM), lambda b, kv, i: (b, 0, kv))

    def values_t(seq):
        return pl.BlockSpec((1, HEAD_DIM, seq), lambda b, kv, i: (b, kv, 0))

    return pl.pallas_call(
        functools.partial(_attn_kernel, tq=tq, tk=tk, n_cblk=lc // tk, n_kblk=(lc + l) // tk),
        grid=(nb, ATTN_KV_HEADS, l // tq),
        in_specs=[pl.BlockSpec((1, tq, gw), lambda b, kv, i: (b, i, kv)),
                  keys(lc), keys(l), values_t(lc), values_t(l)],
        out_specs=pl.BlockSpec((1, tq, gw), lambda b, kv, i: (b, i, kv)),
        out_shape=jax.ShapeDtypeStruct((nb, l, ATTN_HEADS * HEAD_DIM), BF16),
        compiler_params=_params("parallel", "parallel", "parallel"),
        name="gqa_attn",
    )(q, k_c, k_x, vt_c, vt_x)


def _merge_kernel(x_ref, gate_ref, odn_ref, z_ref, oat_ref, g_ref, dnn_ref, post_ref,
                  wdn_ref, wat_ref, wo_ref, o_ref):
    x = x_ref[0]
    d = x.shape[-1]
    y_at = _dot(oat_ref[0], wat_ref[...])
    odn = odn_ref[0].astype(F32)
    zz = z_ref[0].astype(F32)
    parts = []
    for hd in range(DN_HEADS):
        sl = slice(hd * HEAD_DIM, (hd + 1) * HEAD_DIM)
        parts.append((_rms(odn[:, sl], dnn_ref[...]) * jax.nn.silu(zz[:, sl])).astype(BF16))
    y_dn = _dot(jnp.concatenate(parts, axis=1), wdn_ref[...])
    g = g_ref[0].astype(F32)
    y = jax.nn.sigmoid(g[:, :d]) * y_dn + jax.nn.sigmoid(g[:, d:]) * y_at
    yy = _dot(y.astype(BF16), wo_ref[...])
    o_ref[0] = x + gate_ref[0] * _rms(yy, post_ref[...])


def _merge(x, gate, o_dn, z, o_at, gates, dn_norm, post_g, w_dn, w_at, w_o, tm):
    nb, l, d = x.shape

    def tok(n):
        return pl.BlockSpec((1, tm, n), lambda b, i: (b, i, 0))

    def full(a):
        return pl.BlockSpec(a.shape, lambda b, i: (0,) * a.ndim)

    return pl.pallas_call(
        _merge_kernel,
        grid=(nb, l // tm),
        in_specs=[tok(d), pl.BlockSpec((1, 1, d), lambda b, i: (b, 0, 0)), tok(o_dn.shape[-1]), tok(z.shape[-1]),
                  tok(o_at.shape[-1]), tok(gates.shape[-1]), full(dn_norm), full(post_g),
                  full(w_dn), full(w_at), full(w_o)],
        out_specs=tok(d),
        out_shape=jax.ShapeDtypeStruct((nb, l, d), F32),
        compiler_params=_params("parallel", "parallel"),
        name="mixer_merge",
    )(x, gate, o_dn, z, o_at, gates, dn_norm, post_g, w_dn, w_at, w_o)


def _rope_tables(l):
    rows = l // GRID_W
    n_freq = HEAD_DIM // 4
    inv_freq = ROPE_THETA ** (-jnp.arange(n_freq, dtype=F32) / n_freq)
    ang_r = jnp.arange(rows, dtype=jnp.int32).astype(F32)[:, None] * inv_freq
    ang_c = jnp.arange(GRID_W, dtype=jnp.int32).astype(F32)[:, None] * inv_freq
    cr, sr = jnp.repeat(jnp.cos(ang_r), GRID_W, axis=0), jnp.repeat(jnp.sin(ang_r), GRID_W, axis=0)
    cc, sc = jnp.tile(jnp.cos(ang_c), (rows, 1)), jnp.tile(jnp.sin(ang_c), (rows, 1))
    cosf = jnp.concatenate([cr, cr, cc, cc], axis=-1)
    sinf = jnp.concatenate([-sr, sr, -sc, sc], axis=-1)
    return cosf, sinf


def _split_w_in(w_in):
    d = w_in.shape[0]
    dn_qkv = 3 * DN_HEADS * HEAD_DIM
    dn_v = DN_HEADS * HEAD_DIM
    n_ba = 4 * DN_HEADS
    a_q = ATTN_HEADS * HEAD_DIM
    a_kv = ATTN_KV_HEADS * HEAD_DIM
    sizes = (dn_qkv, dn_v, n_ba, a_q, a_kv, a_kv, 2 * d)
    offs = [0]
    for sz in sizes:
        offs.append(offs[-1] + sz)
    assert offs[-1] == w_in.shape[1]
    wb = w_in.astype(BF16)
    part = [wb[:, offs[i]:offs[i + 1]] for i in range(len(sizes))]
    return {"qkv": part[0], "z": part[1], "bat": part[2].T, "q": part[3], "k": part[4],
            "vt": part[5].T, "g": part[6]}


def kernel(x, c, ctx, c_ctx, w_mod, b_mod, ffn1_pre, ffn1_post, ffn1_w_gu, ffn1_w_down, mix_pre, mix_post, w_in, dn_conv, dn_a_log, dn_dt_bias, dn_out_norm, w_dn_out, attn_q_norm, attn_k_norm, w_attn_out, w_out, ffn2_pre, ffn2_post, ffn2_w_gu, ffn2_w_down):
    nb, l, d = x.shape
    lc = ctx.shape[1]
    assert w_mod.shape[0] == 1, "single-layer block: the context stream is never updated"
    assert l % DN_WIN == 0 and lc % DN_WIN == 0 and l % GRID_W == 0
    tm_x = min(512, l)
    tm_c = min(256, lc)

    pad = (-(nb + 1)) % 8
    cc = jnp.concatenate([c, c_ctx[None, :], jnp.zeros((pad, d), F32)], axis=0)
    mod_all = _modulation(cc, w_mod[0], b_mod[0])
    mod = [mod_all[:nb, j * d:(j + 1) * d][:, None, :] for j in range(N_MOD)]
    mod_c = [jnp.broadcast_to(mod_all[nb:nb + 1, j * d:(j + 1) * d][:, None, :], (nb, 1, d)) for j in range(N_MOD)]

    f1 = _ffn_weights(ffn1_w_gu[0], ffn1_w_down[0])
    x1 = _ffn(x, mod[0], mod[1], mod[2], ffn1_pre, ffn1_post, *f1, tm=tm_x)
    c1 = _ffn(ctx, mod_c[0], mod_c[1], mod_c[2], ffn1_pre, ffn1_post, *f1, tm=tm_c)

    w = _split_w_in(w_in[0])
    rope = _rope_tables(l)
    qkv_x, z_x, qa_x, ka_x, vt_x, g_x, bat_x = _inproj(
        x1, mod[3], mod[4], mix_pre, w, attn_q_norm, attn_k_norm, dn_conv[0], rope, tm=min(256, l))
    qkv_c, _, _, ka_c, vt_c, _, bat_c = _inproj(
        c1, mod_c[3], mod_c[4], mix_pre, w, attn_q_norm, attn_k_norm, dn_conv[0], None, tm=tm_c)

    z = lc + l
    bat = jnp.concatenate([bat_c, bat_x], axis=-1)
    rows = bat.reshape(nb, 2, 2, DN_HEADS, z).transpose(0, 3, 1, 2, 4).reshape(nb, DN_HEADS, 4, z)
    rows = jnp.concatenate([rows, jnp.zeros_like(rows)], axis=2)
    o_dn = _deltanet(qkv_x, qkv_c, rows, dn_a_log[0], dn_dt_bias[0])

    o_at = _attention(qa_x, ka_c, ka_x, vt_c, vt_x, tq=min(256, l), tk=DN_WIN)

    x2 = _merge(x1, mod[5], o_dn, z_x, o_at, g_x, dn_out_norm, mix_post,
                w_dn_out[0].astype(BF16), w_attn_out[0].astype(BF16), w_out[0].astype(BF16), tm=tm_x)

    f2 = _ffn_weights(ffn2_w_gu[0], ffn2_w_down[0])
    return _ffn(x2, mod[6], mod[7], mod[8], ffn2_pre, ffn2_post, *f2, tm=tm_x)
```
